```python
import math, functools
import jax, jax.numpy as jnp
from jax import lax
import numpy as np

D_MODEL = 1024
BATCH = 8
SEQ = 2048
DEPTH = 2

EPS = 1e-6
CHUNK = 64
MIX_WIDTH = D_MODEL
A_WIDTH = MIX_WIDTH // 2
A_HEAD_DIM = 128
A_HEADS = A_WIDTH // A_HEAD_DIM
A_CONV = 4
A_IN = 4 * A_WIDTH + 2 * A_HEADS
B_WIDTH = MIX_WIDTH - A_WIDTH
B_HEAD_DIM = 64
B_HEADS = B_WIDTH // B_HEAD_DIM
B_DECAY_LORA = 64
B_AAA_LORA = 64
B_GATE_LORA = 128
B_LN_EPS = 64e-5
B_IN = 3 * B_WIDTH + B_DECAY_LORA + B_AAA_LORA + B_GATE_LORA
C_WIDTH = MIX_WIDTH // 2
C_BLOCKS = 8
C_BLOCK_DIM = C_WIDTH // C_BLOCKS
C_CONV = 4
C_GATE_SCALE = 8.0
D_WIDTH = MIX_WIDTH - C_WIDTH
D_HEADS = 4
D_VAL_DIM = D_WIDTH // D_HEADS
D_KEY_DIM = D_VAL_DIM // 2
D_QK_WIDTH = D_HEADS * D_KEY_DIM
ROPE_BASE = 10000.0
CD_SIZES = (C_WIDTH, C_WIDTH, D_QK_WIDTH, D_QK_WIDTH, D_WIDTH, D_WIDTH)
FFN_DIM = 2816
FFN_CONV = 3

kernel_name = "hybrid_deltanet_rwkv7_rglru_retnet_trunk"


def _split(p, sizes):
    offsets = [int(s) for s in np.cumsum(sizes)[:-1]]
    return jnp.split(p, offsets, axis=-1)


def _rmsnorm(x, w, eps=EPS):
    xf = x.astype(jnp.float32)
    y = xf * lax.rsqrt(jnp.mean(xf * xf, axis=-1, keepdims=True) + eps)
    return (y * w.astype(jnp.float32)).astype(x.dtype)


def _headnorm(x, w, b, eps):
    xf = x.astype(jnp.float32)
    mu = jnp.mean(xf, axis=-1, keepdims=True)
    var = jnp.mean(jnp.square(xf - mu), axis=-1, keepdims=True)
    return (xf - mu) * lax.rsqrt(var + eps) * w.astype(jnp.float32) + b.astype(jnp.float32)


def _l2norm(x, eps=EPS):
    xf = x.astype(jnp.float32)
    return xf * lax.rsqrt(jnp.sum(xf * xf, axis=-1, keepdims=True) + eps)


def _causal_dwconv(x, w):
    k = w.shape[0]
    return lax.conv_general_dilated(
        x, w[:, None, :].astype(x.dtype), window_strides=(1,), padding=[(k - 1, 0)],
        dimension_numbers=("NWC", "WIO", "NWC"), feature_group_count=x.shape[-1])


def _token_shift(p):
    return jnp.pad(p, ((0, 0), (1, 0), (0, 0)))[:, :-1]


def _to_chunks(x):
    b, t, h = x.shape[:3]
    x = x.reshape((b, t // CHUNK, CHUNK, h) + x.shape[3:])
    return jnp.moveaxis(x, 3, 1)


def _from_chunks(x):
    b, h, n, c = x.shape[:4]
    x = jnp.moveaxis(x, 1, 3)
    return x.reshape((b, n * c, h) + x.shape[4:])


def _gated_delta_rule(q, k, v, beta, g):
    dk, dv = q.shape[-1], v.shape[-1]
    bsz, _, h = q.shape[:3]
    qc = _to_chunks(q) * (dk ** -0.5)
    kc = _to_chunks(k)
    vc = _to_chunks(v)
    bc = _to_chunks(beta)
    gc = jnp.cumsum(_to_chunks(g), axis=-1)
    causal = jnp.tril(jnp.ones((CHUNK, CHUNK), dtype=bool))
    strict = jnp.tril(jnp.ones((CHUNK, CHUNK), dtype=bool), -1)
    diff = gc[..., :, None] - gc[..., None, :]
    decay = jnp.where(causal, jnp.exp(jnp.where(causal, diff, 0.0)), 0.0)
    kb = kc * bc[..., None]
    a_mat = jnp.where(strict, jnp.einsum("bhnid,bhnjd->bhnij", kb, kc) * decay, 0.0)
    eye = jnp.eye(CHUNK, dtype=jnp.float32)
    t_mat = lax.linalg.triangular_solve(a_mat + eye, jnp.broadcast_to(eye, a_mat.shape),
                                        left_side=True, lower=True, unit_diagonal=True)
    u = jnp.einsum("bhnij,bhnje->bhnie", t_mat, vc * bc[..., None])
    w = jnp.einsum("bhnij,bhnjd->bhnid", t_mat, kb * jnp.exp(gc)[..., None])
    qk = jnp.where(causal, jnp.einsum("bhnid,bhnjd->bhnij", qc, kc) * decay, 0.0)
    q_dec = qc * jnp.exp(gc)[..., None]
    k_dec = kc * jnp.exp(gc[..., -1:] - gc)[..., None]
    g_last = jnp.exp(gc[..., -1])

    def step(s, inp):
        q_i, k_i, u_i, w_i, qk_i, gl_i = inp
        v_new = u_i - jnp.einsum("bhid,bhde->bhie", w_i, s)
        o = jnp.einsum("bhid,bhde->bhie", q_i, s) + jnp.einsum("bhij,bhje->bhie", qk_i, v_new)
        s = s * gl_i[..., None, None] + jnp.einsum("bhid,bhie->bhde", k_i, v_new)
        return s, o

    xs = tuple(jnp.moveaxis(t, 2, 0) for t in (q_dec, k_dec, u, w, qk, g_last))
    s0 = jnp.zeros((bsz, h, dk, dv), jnp.float32)
    _, o = lax.scan(step, s0, xs)
    return _from_chunks(jnp.moveaxis(o, 0, 2))


def _rwkv7_recurrence(r, decay, k, v, a_vec, b_vec):
    bsz, _, h, n = r.shape

    def step(s, inp):
        r_t, w_t, k_t, v_t, a_t, b_t = inp
        sa = jnp.einsum("bhvk,bhk->bhv", s, a_t)
        s = s * w_t[:, :, None, :] + sa[..., None] * b_t[:, :, None, :] + v_t[..., None] * k_t[:, :, None, :]
        return s, jnp.einsum("bhvk,bhk->bhv", s, r_t)

    xs = tuple(jnp.moveaxis(t, 1, 0) for t in (r, decay, k, v, a_vec, b_vec))
    s0 = jnp.zeros((bsz, h, n, n), jnp.float32)
    _, y = lax.scan(step, s0, xs)
    return jnp.moveaxis(y, 0, 1)


def _rglru(x, wa, ba, wx, bx, lam):
    bsz, t, _ = x.shape
    xf = x.astype(jnp.float32)
    xb = xf.reshape(bsz, t, C_BLOCKS, C_BLOCK_DIM)
    gate_r = jax.nn.sigmoid(jnp.einsum("btgi,gij->btgj", xb, wa).reshape(bsz, t, C_WIDTH) + ba)
    gate_i = jax.nn.sigmoid(jnp.einsum("btgi,gij->btgj", xb, wx).reshape(bsz, t, C_WIDTH) + bx)
    log_a = -C_GATE_SCALE * gate_r * jax.nn.softplus(-lam.astype(jnp.float32))
    a = jnp.exp(log_a)
    drive = jnp.sqrt(-jnp.expm1(2.0 * log_a)) * (gate_i * xf)

    def combine(c1, c2):
        a1, b1 = c1
        a2, b2 = c2
        return a1 * a2, a2 * b1 + b2

    _, hseq = lax.associative_scan(combine, (a, drive), axis=1)
    return hseq


def _rotary(x, pos):
    d = x.shape[-1]
    inv = 1.0 / (ROPE_BASE ** jnp.linspace(0.0, 1.0, d // 2, dtype=jnp.float32))
    ang = pos.astype(jnp.float32)[:, None] * inv[None, :]
    cos = jnp.cos(ang)[None, :, None, :]
    sin = jnp.sin(ang)[None, :, None, :]
    x1, x2 = x[..., 0::2], x[..., 1::2]
    return jnp.stack([x1 * cos - x2 * sin, x2 * cos + x1 * sin], axis=-1).reshape(x.shape)


def _retention(q, k, v):
    bsz, _, h, dk = q.shape
    dv = v.shape[-1]
    log_gamma = jnp.log(1.0 - 2.0 ** (-5.0 - jnp.arange(h, dtype=jnp.float32)))
    idx = jnp.arange(CHUNK, dtype=jnp.float32)
    rel = idx[:, None] - idx[None, :]
    causal = rel >= 0
    dmat = jnp.where(causal[None], jnp.exp(jnp.where(causal, rel, 0.0)[None] * log_gamma[:, None, None]), 0.0)
    qc = _to_chunks(q)
    kc = _to_chunks(k) * (dk ** -0.5)
    vc = _to_chunks(v)
    scores = jnp.einsum("bhnid,bhnjd->bhnij", qc, kc) * dmat[None, :, None]
    o_inner = jnp.einsum("bhnij,bhnje->bhnie", scores, vc)
    q_dec = qc * jnp.exp((idx + 1.0)[None, :] * log_gamma[:, None])[None, :, None, :, None]
    k_dec = kc * jnp.exp((CHUNK - 1.0 - idx)[None, :] * log_gamma[:, None])[None, :, None, :, None]
    chunk_decay = jnp.exp(CHUNK * log_gamma)[None, :, None, None]

    def step(state, inp):
        q_i, k_i, v_i = inp
        o = jnp.einsum("bhid,bhde->bhie", q_i, state)
        state = state * chunk_decay + jnp.einsum("bhid,bhie->bhde", k_i, v_i)
        return state, o

    xs = tuple(jnp.moveaxis(t, 2, 0) for t in (q_dec, k_dec, vc))
    s0 = jnp.zeros((bsz, h, dk, dv), jnp.float32)
    _, o_cross = lax.scan(step, s0, xs)
    return _from_chunks(o_inner + jnp.moveaxis(o_cross, 0, 2))


def _mixer_ab(h, w_in, a_conv, a_A_log, a_dt_bias, a_norm, b_mu, b_w0, b_w2, b_a0, b_a2, b_g2,
              b_k_k, b_k_a, b_r_k, b_ln_w, b_ln_b, w_out):
    bsz, t, _ = h.shape
    p = h @ w_in
    p_a, p_b = p[..., :A_IN], p[..., A_IN:]
    qkv, z, beta_in, alpha_in = _split(p_a, (3 * A_WIDTH, A_WIDTH, A_HEADS, A_HEADS))
    qkv = jax.nn.silu(_causal_dwconv(qkv, a_conv))
    q, k, v = (u.reshape(bsz, t, A_HEADS, A_HEAD_DIM) for u in jnp.split(qkv, 3, axis=-1))
    beta = jax.nn.sigmoid(beta_in.astype(jnp.float32))
    g = -jnp.exp(a_A_log.astype(jnp.float32)) * jax.nn.softplus(alpha_in.astype(jnp.float32) + a_dt_bias)
    o_a = _gated_delta_rule(_l2norm(q), _l2norm(k), v.astype(jnp.float32), beta, g)
    o_a = _rmsnorm(o_a, a_norm) * jax.nn.silu(z.astype(jnp.float32).reshape(bsz, t, A_HEADS, A_HEAD_DIM))
    o_a = o_a.reshape(bsz, t, A_WIDTH).astype(h.dtype)
    p_b = p_b + (_token_shift(p_b) - p_b) * b_mu
    r, kr, vr, w_lo, a_lo, g_lo = _split(p_b, (B_WIDTH, B_WIDTH, B_WIDTH, B_DECAY_LORA, B_AAA_LORA, B_GATE_LORA))
    w_log = -jax.nn.softplus(-(b_w0 + jnp.tanh(w_lo) @ b_w2)) - 0.5
    decay = jnp.exp(-jnp.exp(w_log.astype(jnp.float32)))
    a_lr = jax.nn.sigmoid(b_a0 + a_lo @ b_a2)
    gate = jax.nn.sigmoid(g_lo) @ b_g2

    def heads(u):
        return u.astype(jnp.float32).reshape(bsz, t, B_HEADS, B_HEAD_DIM)

    kk = _l2norm(heads(kr * b_k_k))
    k_mod = heads(kr * (1.0 + (a_lr - 1.0) * b_k_a))
    rh, vh, ah = heads(r), heads(vr), heads(a_lr)
    y = _rwkv7_recurrence(rh, heads(decay), k_mod, vh, -kk, kk * ah)
    y = _headnorm(y, b_ln_w.reshape(B_HEADS, B_HEAD_DIM), b_ln_b.reshape(B_HEADS, B_HEAD_DIM), B_LN_EPS)
    y = y + jnp.sum(rh * k_mod * b_r_k, axis=-1, keepdims=True) * vh
    o_b = (y.reshape(bsz, t, B_WIDTH) * gate).astype(h.dtype)
    return jnp.concatenate([o_a, o_b], axis=-1) @ w_out


def _mixer_cd(h, w_in, c_conv_w, c_conv_b, c_wa, c_ba, c_wx, c_bx, c_lambda, d_gn_w, d_gn_b, w_out):
    bsz, t, _ = h.shape
    gate_c, x_c, q, k, v, g_d = _split(h @ w_in, CD_SIZES)
    x_c = _causal_dwconv(x_c, c_conv_w) + c_conv_b
    h_c = _rglru(x_c, c_wa, c_ba, c_wx, c_bx, c_lambda)
    o_c = (h_c * jax.nn.gelu(gate_c.astype(jnp.float32))).astype(h.dtype)
    pos = jnp.arange(t)
    q = _rotary(q.astype(jnp.float32).reshape(bsz, t, D_HEADS, D_KEY_DIM), pos)
    k = _rotary(k.astype(jnp.float32).reshape(bsz, t, D_HEADS, D_KEY_DIM), pos)
    o_d = _retention(q, k, v.astype(jnp.float32).reshape(bsz, t, D_HEADS, D_VAL_DIM))
    o_d = _headnorm(o_d, d_gn_w.reshape(D_HEADS, D_VAL_DIM), d_gn_b.reshape(D_HEADS, D_VAL_DIM), EPS)
    o_d = (jax.nn.silu(g_d.astype(jnp.float32)) * o_d.reshape(bsz, t, D_WIDTH)).astype(h.dtype)
    return jnp.concatenate([o_c, o_d], axis=-1) @ w_out


def _conv_glu(h, w_up, w_conv, w_down):
    u, v = jnp.split(h @ w_up, 2, axis=-1)
    u = _causal_dwconv(u, w_conv)
    return (jax.nn.silu(u) * v) @ w_down


def setup_inputs(seed: int = 0) -> dict:
    key = jax.random.key(seed)
    ks = jax.random.split(key, 48)
    cnt = [0]

    def nxt():
        cnt[0] += 1
        return ks[cnt[0] - 1]

    def nrm(shape, scale):
        return scale * jax.random.normal(nxt(), shape, jnp.float32)

    def uni(shape, lo, hi):
        return jax.random.uniform(nxt(), shape, jnp.float32, lo, hi)

    def gain(n):
        return 1.0 + nrm((n,), 0.02)

    inp = {}
    inp["x"] = nrm((BATCH, SEQ, D_MODEL), 1.0)
    inp["l0_norm1"] = gain(D_MODEL)
    inp["l0_w_in"] = nrm((D_MODEL, A_IN + B_IN), D_MODEL ** -0.5)
    inp["l0_a_conv"] = nrm((A_CONV, 3 * A_WIDTH), A_CONV ** -0.5)
    inp["l0_a_A_log"] = jnp.log(uni((A_HEADS,), 1.0, 16.0))
    dt = jnp.exp(uni((A_HEADS,), math.log(1e-3), math.log(1e-1)))
    inp["l0_a_dt_bias"] = dt + jnp.log(-jnp.expm1(-dt))
    inp["l0_a_norm"] = gain(A_HEAD_DIM)
    inp["l0_b_mu"] = uni((B_IN,), 0.0, 1.0)
    inp["l0_b_w0"] = uni((B_WIDTH,), -6.0, -1.0)
    inp["l0_b_w2"] = nrm((B_DECAY_LORA, B_WIDTH), 0.1)
    inp["l0_b_a0"] = nrm((B_WIDTH,), 0.1)
    inp["l0_b_a2"] = nrm((B_AAA_LORA, B_WIDTH), 0.1)
    inp["l0_b_g2"] = nrm((B_GATE_LORA, B_WIDTH), B_GATE_LORA ** -0.5)
    inp["l0_b_k_k"] = 0.85 + nrm((B_WIDTH,), 0.02)
    inp["l0_b_k_a"] = 1.0 + nrm((B_WIDTH,), 0.02)
    inp["l0_b_r_k"] = nrm((B_HEADS, B_HEAD_DIM), 0.1)
    inp["l0_b_ln_w"] = gain(B_WIDTH)
    inp["l0_b_ln_b"] = nrm((B_WIDTH,), 0.02)
    inp["l0_w_out"] = nrm((MIX_WIDTH, D_MODEL), MIX_WIDTH ** -0.5)
    inp["l0_norm2"] = gain(D_MODEL)
    inp["l0_ffn_up"] = nrm((D_MODEL, 2 * FFN_DIM), D_MODEL ** -0.5)
    inp["l0_ffn_conv"] = nrm((FFN_CONV, FFN_DIM), FFN_CONV ** -0.5)
    inp["l0_ffn_down"] = nrm((FFN_DIM, D_MODEL), FFN_DIM ** -0.5)
    inp["l1_norm1"] = gain(D_MODEL)
    inp["l1_w_in"] = nrm((D_MODEL, sum(CD_SIZES)), D_MODEL ** -0.5)
    inp["l1_c_conv_w"] = nrm((C_CONV, C_WIDTH), C_CONV ** -0.5)
    inp["l1_c_conv_b"] = nrm((C_WIDTH,), 0.02)
    inp["l1_c_wa"] = nrm((C_BLOCKS, C_BLOCK_DIM, C_BLOCK_DIM), C_BLOCK_DIM ** -0.5)
    inp["l1_c_ba"] = nrm((C_WIDTH,), 0.02)
    inp["l1_c_wx"] = nrm((C_BLOCKS, C_BLOCK_DIM, C_BLOCK_DIM), C_BLOCK_DIM ** -0.5)
    inp["l1_c_bx"] = nrm((C_WIDTH,), 0.02)
    a_c = uni((C_WIDTH,), 0.9, 0.999) ** (1.0 / C_GATE_SCALE)
    inp["l1_c_lambda"] = jnp.log(a_c) - jnp.log1p(-a_c)
    inp["l1_d_gn_w"] = gain(D_WIDTH)
    inp["l1_d_gn_b"] = nrm((D_WIDTH,), 0.02)
    inp["l1_w_out"] = nrm((MIX_WIDTH, D_MODEL), MIX_WIDTH ** -0.5)
    inp["l1_norm2"] = gain(D_MODEL)
    inp["l1_ffn_up"] = nrm((D_MODEL, 2 * FFN_DIM), D_MODEL ** -0.5)
    inp["l1_ffn_conv"] = nrm((FFN_CONV, FFN_DIM), FFN_CONV ** -0.5)
    inp["l1_ffn_down"] = nrm((FFN_DIM, D_MODEL), FFN_DIM ** -0.5)
    inp["final_norm"] = gain(D_MODEL)
    return inp


def reference(x, l0_norm1, l0_w_in, l0_a_conv, l0_a_A_log, l0_a_dt_bias, l0_a_norm, l0_b_mu, l0_b_w0,
              l0_b_w2, l0_b_a0, l0_b_a2, l0_b_g2, l0_b_k_k, l0_b_k_a, l0_b_r_k, l0_b_ln_w, l0_b_ln_b,
              l0_w_out, l0_norm2, l0_ffn_up, l0_ffn_conv, l0_ffn_down,
              l1_norm1, l1_w_in, l1_c_conv_w, l1_c_conv_b, l1_c_wa, l1_c_ba, l1_c_wx, l1_c_bx,
              l1_c_lambda, l1_d_gn_w, l1_d_gn_b, l1_w_out, l1_norm2, l1_ffn_up, l1_ffn_conv,
              l1_ffn_down, final_norm):
    mixers = (
        functools.partial(_mixer_ab, w_in=l0_w_in, a_conv=l0_a_conv, a_A_log=l0_a_A_log,
                          a_dt_bias=l0_a_dt_bias, a_norm=l0_a_norm, b_mu=l0_b_mu, b_w0=l0_b_w0,
                          b_w2=l0_b_w2, b_a0=l0_b_a0, b_a2=l0_b_a2, b_g2=l0_b_g2, b_k_k=l0_b_k_k,
                          b_k_a=l0_b_k_a, b_r_k=l0_b_r_k, b_ln_w=l0_b_ln_w, b_ln_b=l0_b_ln_b,
                          w_out=l0_w_out),
        functools.partial(_mixer_cd, w_in=l1_w_in, c_conv_w=l1_c_conv_w, c_conv_b=l1_c_conv_b,
                          c_wa=l1_c_wa, c_ba=l1_c_ba, c_wx=l1_c_wx, c_bx=l1_c_bx,
                          c_lambda=l1_c_lambda, d_gn_w=l1_d_gn_w, d_gn_b=l1_d_gn_b,
                          w_out=l1_w_out),
    )
    ffns = (
        functools.partial(_conv_glu, w_up=l0_ffn_up, w_conv=l0_ffn_conv, w_down=l0_ffn_down),
        functools.partial(_conv_glu, w_up=l1_ffn_up, w_conv=l1_ffn_conv, w_down=l1_ffn_down),
    )
    mixer_norms = (l0_norm1, l1_norm1)
    ffn_norms = (l0_norm2, l1_norm2)
    for i in range(DEPTH):
        x = x + mixers[i](_rmsnorm(x, mixer_norms[i]))
        x = x + ffns[i](_rmsnorm(x, ffn_norms[i]))
    return _rmsnorm(x, final_norm)
```

```python
import functools
import math

import numpy as np
import jax
import jax.numpy as jnp
from jax import lax
from jax.experimental import pallas as pl
from jax.experimental.pallas import tpu as pltpu

F32 = jnp.float32
BF16 = jnp.bfloat16

EPS = 1e-6
CHUNK = 64
A_HEAD_DIM = 128
A_CONV = 4
B_HEAD_DIM = 64
B_LN_EPS = 64e-5
B_LORA_PAD = 128
C_GATE_SCALE = 8.0
C_CONV = 4
D_KEY_DIM = 64
D_VAL_DIM = 128
D_HEADS = 4
ROPE_BASE = 10000.0
FFN_CONV = 3

LANES = 128
SUBLANES = 8
BF16_ROWS = 16
VMEM_LIMIT = 56 * 1024 * 1024

MIX_ROWS = 256
PROJ_ROWS = 256
FFN_ROWS = 512
FFN_SPLIT = 2

NN = (((1,), (0,)), ((), ()))
NT = (((1,), (1,)), ((), ()))
TN = (((0,), (0,)), ((), ()))


def _pieces(x, n):
    if x.dtype == BF16:
        return [x]
    out, r = [], x
    for i in range(n):
        p = r.astype(BF16)
        out.append(p)
        if i + 1 < n:
            r = r - p.astype(F32)
    return out


def _dot(a, b, dims=NN, pa=1, pb=1):
    ap, bp = _pieces(a, pa), _pieces(b, pb)
    order = max(len(ap), len(bp))
    acc = None
    for i in reversed(range(len(ap))):
        for j in reversed(range(len(bp))):
            if i + j < order:
                t = lax.dot_general(ap[i], bp[j], dims, preferred_element_type=F32)
                acc = t if acc is None else acc + t
    return acc


def _sigmoid(x):
    return 1.0 / (1.0 + jnp.exp(-x))


def _silu(x):
    return x * _sigmoid(x)


def _softplus(x):
    return jnp.maximum(x, 0.0) + jnp.log1p(jnp.exp(-jnp.abs(x)))


def _gelu_tanh(x):
    return 0.5 * x * (1.0 + jnp.tanh(math.sqrt(2.0 / math.pi) * (x + 0.044715 * (x * x * x))))


def _rms(x, g):
    return x * lax.rsqrt(jnp.mean(x * x, axis=-1, keepdims=True) + EPS) * g


def _iota(shape, dim):
    return lax.broadcasted_iota(jnp.int32, shape, dim)


def _div(i, n):
    assert n & (n - 1) == 0
    return i >> (n.bit_length() - 1)


def _mod(i, n):
    assert n & (n - 1) == 0
    return i & (n - 1)


def _pair_masks(n):
    i, j = _iota((n, n), 0), _iota((n, n), 1)
    same = _div(i, CHUNK) == _div(j, CHUNK)
    return same & (_mod(j, CHUNK) < _mod(i, CHUNK)), same & (_mod(j, CHUNK) <= _mod(i, CHUNK))


def _inv_unit_lower(n_mat, passes):
    size = n_mat.shape[0]
    eye = (_iota((size, size), 0) == _iota((size, size), 1)).astype(F32)
    t = eye + n_mat
    p = n_mat
    for _ in range(int(math.log2(CHUNK)) - 1):
        p = _dot(p, p, pa=passes, pb=passes)
        t = t + _dot(t, p, pa=passes, pb=passes)
    return t


def _stack_pair(x, half):
    first = _iota(x.shape, 1) < half
    return jnp.concatenate([jnp.where(first, x, 0.0), jnp.where(first, 0.0, x)], axis=0)


def _shifted_rows(xp_ref, back, rows):
    return xp_ref[pl.ds(SUBLANES - back, rows), :]


def _params(sem):
    return pltpu.CompilerParams(dimension_semantics=sem, vmem_limit_bytes=VMEM_LIMIT)


def _const_spec(shape):
    return pl.BlockSpec(shape, lambda *_: (0,) * len(shape))


def _norm_proj_kernel(x_ref, g_ref, w_ref, *o_refs, splits):
    h = _rms(x_ref[...], g_ref[...]).astype(BF16)
    off = 0
    for o_ref, n in zip(o_refs, splits):
        o_ref[...] = jnp.dot(h, w_ref[:, off:off + n], preferred_element_type=F32)
        off += n


def _norm_proj(x, g, w, splits, name):
    m, d = x.shape
    tm = PROJ_ROWS
    return pl.pallas_call(
        functools.partial(_norm_proj_kernel, splits=splits),
        grid=(m // tm,),
        in_specs=[pl.BlockSpec((tm, d), lambda i: (i, 0)), _const_spec((1, d)), _const_spec(w.shape)],
        out_specs=[pl.BlockSpec((tm, s), lambda i: (i, 0)) for s in splits],
        out_shape=[jax.ShapeDtypeStruct((m, s), F32) for s in splits],
        compiler_params=_params(("parallel",)),
        name=name,
    )(x, g.reshape(1, d), w)


def _gdn_kernel(qkvz_ref, ba_ref, cw_ref, alog_ref, dtb_ref, nw_ref, bd_ref, ex_ref, ltri_ref, lblk_ref,
                o_ref, xp_ref, s_ref, qs_ref, k_ref, kb_ref, vb_ref, kbg_ref, qd_ref, kd_ref, gc_ref,
                gl_ref, oc_ref, *, inv_passes):
    rows = qkvz_ref.shape[0]
    width = o_ref.shape[1]
    tb = pl.program_id(1)

    @pl.when(tb == 0)
    def _():
        xp_ref[0:SUBLANES, :] = jnp.zeros((SUBLANES, xp_ref.shape[1]), F32)
        s_ref[...] = jnp.zeros(s_ref.shape, F32)

    xp_ref[SUBLANES:SUBLANES + rows, :] = qkvz_ref[:, 0:3 * width]
    acc = cw_ref[A_CONV - 1:A_CONV, :] * _shifted_rows(xp_ref, 0, rows)
    for back in range(1, A_CONV):
        acc = acc + cw_ref[A_CONV - 1 - back:A_CONV - back, :] * _shifted_rows(xp_ref, back, rows)
    xp_ref[0:SUBLANES, :] = xp_ref[rows:rows + SUBLANES, :]
    act = _silu(acc)
    q, k, v = act[:, 0:width], act[:, width:2 * width], act[:, 2 * width:3 * width]

    def l2n(x):
        return x * lax.rsqrt(_dot(x * x, bd_ref[...], pa=2) + EPS)

    ba = ba_ref[...]
    n_heads = width // A_HEAD_DIM
    bg = jnp.where(_iota(ba.shape, 1) < n_heads, _sigmoid(ba),
                   -jnp.exp(alog_ref[...]) * _softplus(ba + dtb_ref[...]))
    cum = _dot(ltri_ref[...], bg, pb=3)
    tot = _dot(lblk_ref[...], bg, pb=3)
    beta_f = _dot(bg, ex_ref[:, 0:width], pa=3)
    gc_f = _dot(cum, ex_ref[:, width:2 * width], pa=3)
    gl_f = _dot(tot, ex_ref[:, width:2 * width], pa=3)

    kn = l2n(k)
    qs = l2n(q) * (A_HEAD_DIM ** -0.5)
    kb = kn * beta_f
    e_gc = jnp.exp(gc_f)
    qs_ref[...] = qs
    k_ref[...] = kn
    kb_ref[...] = kb
    vb_ref[...] = v * beta_f
    kbg_ref[...] = kb * e_gc
    qd_ref[...] = qs * e_gc
    kd_ref[...] = kn * jnp.exp(gl_f - gc_f)
    gc_ref[...] = gc_f
    gl_ref[...] = gl_f

    pair_w = 2 * A_HEAD_DIM
    pair_t = 2 * CHUNK
    strict, lower = _pair_masks(pair_t)
    dotp = functools.partial(_dot, pa=inv_passes, pb=inv_passes)

    def chunk_body(c, carry):
        r0 = pl.multiple_of(c * CHUNK, CHUNK)
        for p in range(width // pair_w):
            lanes = slice(p * pair_w, (p + 1) * pair_w)

            def ld(ref):
                return ref[pl.ds(r0, CHUNK), lanes]

            def st(ref):
                return _stack_pair(ld(ref), A_HEAD_DIM)

            ks = st(k_ref)
            g = dotp(jnp.concatenate([st(kb_ref), st(qs_ref)], axis=0), ks, NT)
            gcp = ld(gc_ref)
            gcol = jnp.concatenate([gcp[:, 0:A_HEAD_DIM], gcp[:, A_HEAD_DIM:pair_w]], axis=0)
            dec = jnp.exp(jnp.where(lower, gcol - gcol.T, 0.0))
            a_m = jnp.where(strict, g[0:pair_t] * dec, 0.0)
            qk_m = jnp.where(lower, g[pair_t:2 * pair_t] * dec, 0.0)
            t_m = _inv_unit_lower(-a_m, inv_passes)
            uw = dotp(t_m, jnp.concatenate([st(vb_ref), st(kbg_ref)], axis=1))
            s = s_ref[p]
            x = dotp(jnp.concatenate([uw[:, pair_w:2 * pair_w], st(qd_ref)], axis=0), s)
            v_new = uw[:, 0:pair_w] - x[0:pair_t]
            o_s = x[pair_t:2 * pair_t] + dotp(qk_m, v_new)
            g_last = jnp.exp(gl_ref[pl.ds(r0, 1), lanes])
            s_ref[p] = s * g_last + dotp(st(kd_ref), v_new, TN)
            oc_ref[pl.ds(r0, CHUNK), lanes] = o_s[0:CHUNK] + o_s[CHUNK:pair_t]
        return carry

    lax.fori_loop(0, rows // CHUNK, chunk_body, 0)

    o = oc_ref[...]
    ms = _dot(o * o, bd_ref[...], pa=2) * (1.0 / A_HEAD_DIM)
    z = qkvz_ref[:, 3 * width:4 * width]
    o_ref[...] = (o * lax.rsqrt(ms + EPS) * nw_ref[...] * _silu(z)).astype(o_ref.dtype)


def _gdn(qkvz, ba, conv_w, a_log, dt_bias, norm_w, batch, inv_passes):
    m = qkvz.shape[0]
    width = qkvz.shape[1] // 4
    n_heads = width // A_HEAD_DIM
    rows = MIX_ROWS
    tblocks = m // batch // rows
    pad = lambda p: jnp.zeros((1, LANES), F32).at[0, n_heads:2 * n_heads].set(p)
    lane_head = np.arange(width) // A_HEAD_DIM
    bd = jnp.asarray(lane_head[:, None] == lane_head[None, :], BF16)
    ex = np.zeros((LANES, 2 * width), np.float32)
    for h in range(n_heads):
        ex[h, h * A_HEAD_DIM:(h + 1) * A_HEAD_DIM] = 1.0
        ex[n_heads + h, width + h * A_HEAD_DIM:width + (h + 1) * A_HEAD_DIM] = 1.0
    t = np.arange(rows)
    same = (t[:, None] // CHUNK) == (t[None, :] // CHUNK)
    ltri = jnp.asarray(same & (t[None, :] <= t[:, None]), BF16)
    lblk = jnp.asarray(same, BF16)
    row_spec = lambda w: pl.BlockSpec((rows, w), lambda b, i: (b * tblocks + i, 0))
    act = lambda: pltpu.VMEM((rows, width), F32)
    return pl.pallas_call(
        functools.partial(_gdn_kernel, inv_passes=inv_passes),
        grid=(batch, tblocks),
        in_specs=[row_spec(4 * width), row_spec(LANES), _const_spec(conv_w.shape), _const_spec((1, LANES)),
                  _const_spec((1, LANES)), _const_spec((1, width)), _const_spec(bd.shape),
                  _const_spec(ex.shape), _const_spec(ltri.shape), _const_spec(lblk.shape)],
        out_specs=row_spec(width),
        out_shape=jax.ShapeDtypeStruct((m, width), BF16),
        scratch_shapes=[pltpu.VMEM((rows + SUBLANES, 3 * width), F32),
                        pltpu.VMEM((n_heads // 2, 2 * A_HEAD_DIM, 2 * A_HEAD_DIM), F32)]
                       + [act() for _ in range(10)],
        compiler_params=_params(("arbitrary", "arbitrary")),
        name="gdn_mixer",
    )(qkvz, ba, conv_w, pad(a_log), pad(dt_bias), jnp.tile(norm_w, n_heads).reshape(1, width), bd,
      jnp.asarray(ex, BF16), ltri, lblk)


def _rwkv_kernel(pb_ref, mu_ref, w0_ref, w2_ref, a0_ref, a2_ref, g2_ref, kk_ref, ka_ref, rk_ref, lnw_ref,
                 lnb_ref, bd_ref, ltri_ref, lblk_ref, o_ref, xp_ref, s_ref, at_ref, rt_ref, bt_ref, kt_ref,
                 v_ref, bdc_ref, kdc_ref, pl_ref, y_ref, bonus_ref, gate_ref, *, inv_passes):
    rows = pb_ref.shape[0]
    width = o_ref.shape[1]
    tb = pl.program_id(1)

    @pl.when(tb == 0)
    def _():
        xp_ref[0:SUBLANES, :] = jnp.zeros((SUBLANES, xp_ref.shape[1]), F32)
        s_ref[...] = jnp.zeros(s_ref.shape, F32)

    p = pb_ref[...]
    xp_ref[SUBLANES:SUBLANES + rows, :] = p
    prev = _shifted_rows(xp_ref, 1, rows)
    xp_ref[0:SUBLANES, :] = xp_ref[rows:rows + SUBLANES, :]
    m = p + (prev - p) * mu_ref[...]
    r, kr, vr = m[:, 0:width], m[:, width:2 * width], m[:, 2 * width:3 * width]
    o1 = 3 * width
    w_lo = m[:, o1:o1 + B_LORA_PAD]
    a_lo = m[:, o1 + B_LORA_PAD:o1 + 2 * B_LORA_PAD]
    g_lo = m[:, o1 + 2 * B_LORA_PAD:]

    w_log = -_softplus(-(w0_ref[...] + _dot(jnp.tanh(w_lo), w2_ref[...]))) - 0.5
    lw = -jnp.exp(w_log)
    a_lr = _sigmoid(a0_ref[...] + _dot(a_lo, a2_ref[...]))
    gate_ref[...] = _dot(_sigmoid(g_lo), g2_ref[...])

    def gsum(x):
        return _dot(x, bd_ref[...], pa=2)

    kk = kr * kk_ref[...]
    kk = kk * lax.rsqrt(gsum(kk * kk) + EPS)
    k_mod = kr * (1.0 + (a_lr - 1.0) * ka_ref[...])
    a_vec = -kk
    b_vec = kk * a_lr
    bonus_ref[...] = gsum(r * k_mod * rk_ref[...]) * vr

    cum = _dot(ltri_ref[...], lw, pb=3)
    tot = _dot(lblk_ref[...], lw, pb=3)
    e_neg = jnp.exp(-cum)
    e_dec = jnp.exp(tot - cum)
    rt_ref[...] = r * jnp.exp(cum)
    at_ref[...] = a_vec * jnp.exp(cum - lw)
    bt_ref[...] = b_vec * e_neg
    kt_ref[...] = k_mod * e_neg
    bdc_ref[...] = b_vec * e_dec
    kdc_ref[...] = k_mod * e_dec
    pl_ref[...] = jnp.exp(tot)
    v_ref[...] = vr

    pair_w = 2 * B_HEAD_DIM
    pair_t = 2 * CHUNK
    strict, lower = _pair_masks(pair_t)

    def chunk_body(c, carry):
        r0 = pl.multiple_of(c * CHUNK, CHUNK)
        for pr in range(width // pair_w):
            lanes = slice(pr * pair_w, (pr + 1) * pair_w)

            def st(ref):
                return _stack_pair(ref[pl.ds(r0, CHUNK), lanes], B_HEAD_DIM)

            a_s, r_s, v_s = st(at_ref), st(rt_ref), st(v_ref)
            g = _dot(jnp.concatenate([a_s, r_s], axis=0),
                     jnp.concatenate([st(bt_ref), st(kt_ref)], axis=0), NT,
                     pa=inv_passes, pb=inv_passes)
            l_ab = jnp.where(strict, g[0:pair_t, 0:pair_t], 0.0)
            l_ak = jnp.where(strict, g[0:pair_t, pair_t:2 * pair_t], 0.0)
            m_rb = jnp.where(lower, g[pair_t:2 * pair_t, 0:pair_t], 0.0)
            m_rk = jnp.where(lower, g[pair_t:2 * pair_t, pair_t:2 * pair_t], 0.0)
            t_m = _inv_unit_lower(l_ab, inv_passes)
            w1 = _dot(l_ak, v_s, pa=inv_passes, pb=inv_passes)
            taw = _dot(t_m, jnp.concatenate([a_s, w1], axis=1), pa=inv_passes, pb=inv_passes)
            s = s_ref[pr]
            x = _dot(jnp.concatenate([taw[:, 0:pair_w], r_s], axis=0), s, NT,
                     pa=inv_passes, pb=inv_passes)
            u_s = x[0:pair_t] + taw[:, pair_w:2 * pair_w]
            uv = jnp.concatenate([u_s, v_s], axis=0)
            y_s = x[pair_t:2 * pair_t] + _dot(jnp.concatenate([m_rb, m_rk], axis=1), uv,
                                              pa=inv_passes, pb=inv_passes)
            p_last = pl_ref[pl.ds(r0, 1), lanes]
            s_ref[pr] = s * p_last + _dot(uv, jnp.concatenate([st(bdc_ref), st(kdc_ref)], axis=0), TN,
                                          pa=inv_passes, pb=inv_passes)
            y_ref[pl.ds(r0, CHUNK), lanes] = y_s[0:CHUNK] + y_s[CHUNK:pair_t]
        return carry

    lax.fori_loop(0, rows // CHUNK, chunk_body, 0)

    y = y_ref[...]
    inv_n = 1.0 / B_HEAD_DIM
    mu = gsum(y) * inv_n
    yc = y - mu
    var = gsum(yc * yc) * inv_n
    yn = yc * lax.rsqrt(var + B_LN_EPS) * lnw_ref[...] + lnb_ref[...]
    o_ref[...] = ((yn + bonus_ref[...]) * gate_ref[...]).astype(o_ref.dtype)


def _rwkv(pb, mu, w0, w2, a0, a2, g2, k_k, k_a, r_k, ln_w, ln_b, batch, inv_passes):
    m, in_w = pb.shape
    width = w0.shape[0]
    rows = MIX_ROWS
    tblocks = m // batch // rows
    lane_head = np.arange(width) // B_HEAD_DIM
    bd = jnp.asarray(lane_head[:, None] == lane_head[None, :], BF16)
    t = np.arange(rows)
    same = (t[:, None] // CHUNK) == (t[None, :] // CHUNK)
    ltri = jnp.asarray(same & (t[None, :] <= t[:, None]), BF16)
    lblk = jnp.asarray(same, BF16)
    row = lambda p: p.reshape(1, -1).astype(F32)
    row_spec = lambda w: pl.BlockSpec((rows, w), lambda b, i: (b * tblocks + i, 0))
    vec = _const_spec((1, width))
    act = lambda: pltpu.VMEM((rows, width), F32)
    return pl.pallas_call(
        functools.partial(_rwkv_kernel, inv_passes=inv_passes),
        grid=(batch, tblocks),
        in_specs=[row_spec(in_w), _const_spec((1, in_w)), vec, _const_spec(w2.shape), vec,
                  _const_spec(a2.shape), _const_spec(g2.shape), vec, vec, vec, vec, vec,
                  _const_spec(bd.shape), _const_spec(ltri.shape), _const_spec(lblk.shape)],
        out_specs=row_spec(width),
        out_shape=jax.ShapeDtypeStruct((m, width), BF16),
        scratch_shapes=[pltpu.VMEM((rows + SUBLANES, in_w), F32),
                        pltpu.VMEM((width // (2 * B_HEAD_DIM), 2 * B_HEAD_DIM, 2 * B_HEAD_DIM), F32)]
                       + [act() for _ in range(11)],
        compiler_params=_params(("arbitrary", "arbitrary")),
        name="rwkv_mixer",
    )(pb, row(mu), row(w0), w2, row(a0), a2, g2, row(k_k), row(k_a), row(r_k), row(ln_w), row(ln_b),
      bd, ltri, lblk)


def _rglru_kernel(gx_ref, cw_ref, cb_ref, wa_ref, ba_ref, wx_ref, bx_ref, lam_ref, o_ref, xp_ref, h_ref):
    rows = gx_ref.shape[0]
    width = o_ref.shape[1]
    tb = pl.program_id(1)

    @pl.when(tb == 0)
    def _():
        xp_ref[0:SUBLANES, :] = jnp.zeros((SUBLANES, width), F32)
        h_ref[...] = jnp.zeros(h_ref.shape, F32)

    xp_ref[SUBLANES:SUBLANES + rows, :] = gx_ref[:, width:2 * width]
    xc = cb_ref[...] + cw_ref[C_CONV - 1:C_CONV, :] * _shifted_rows(xp_ref, 0, rows)
    for back in range(1, C_CONV):
        xc = xc + cw_ref[C_CONV - 1 - back:C_CONV - back, :] * _shifted_rows(xp_ref, back, rows)
    xp_ref[0:SUBLANES, :] = xp_ref[rows:rows + SUBLANES, :]

    gate_r = _sigmoid(_dot(xc, wa_ref[...]) + ba_ref[...])
    gate_i = _sigmoid(_dot(xc, wx_ref[...]) + bx_ref[...])
    log_a = -C_GATE_SCALE * gate_r * _softplus(-lam_ref[...])
    a = jnp.exp(log_a)
    d = jnp.sqrt(jnp.tanh(-log_a) * (a * a + 1.0)) * (gate_i * xc)

    t_idx = _iota((rows, width), 0)
    shift = 1
    while shift < rows:
        keep = t_idx >= shift
        a_sh = pltpu.roll(a, shift, 0)
        d_sh = pltpu.roll(d, shift, 0)
        d = jnp.where(keep, a * d_sh + d, d)
        a = jnp.where(keep, a * a_sh, a)
        shift *= 2
    h = a * h_ref[0:1, :] + d
    h_ref[...] = h[rows - SUBLANES:rows, :][SUBLANES - 1:SUBLANES, :] * jnp.ones(h_ref.shape, F32)
    o_ref[...] = (h * _gelu_tanh(gx_ref[:, 0:width])).astype(o_ref.dtype)


def _rglru(gx, conv_w, conv_b, wa, ba, wx, bx, lam, batch):
    m = gx.shape[0]
    width = gx.shape[1] // 2
    rows = MIX_ROWS
    tblocks = m // batch // rows
    row = lambda p: p.reshape(1, -1).astype(F32)
    blockdiag = lambda w: jax.scipy.linalg.block_diag(*[w[i] for i in range(w.shape[0])]).astype(BF16)
    row_spec = lambda w: pl.BlockSpec((rows, w), lambda b, i: (b * tblocks + i, 0))
    vec = _const_spec((1, width))
    mat = _const_spec((width, width))
    return pl.pallas_call(
        _rglru_kernel,
        grid=(batch, tblocks),
        in_specs=[row_spec(2 * width), _const_spec(conv_w.shape), vec, mat, vec, mat, vec, vec],
        out_specs=row_spec(width),
        out_shape=jax.ShapeDtypeStruct((m, width), BF16),
        scratch_shapes=[pltpu.VMEM((rows + SUBLANES, width), F32), pltpu.VMEM((SUBLANES, width), F32)],
        compiler_params=_params(("arbitrary", "arbitrary")),
        name="rglru_mixer",
    )(gx, conv_w, row(conv_b), blockdiag(wa), row(ba), blockdiag(wx), row(bx), row(lam))


_LOG_GAMMA = tuple(math.log(1.0 - 2.0 ** (-5.0 - h)) for h in range(D_HEADS))


def _per_head(lane_head, values):
    out = jnp.full(lane_head.shape, values[-1], F32)
    for h in range(len(values) - 2, -1, -1):
        out = jnp.where(lane_head == h, values[h], out)
    return out


def _retention_kernel(qk_ref, vg_ref, cos_ref, sin_ref, gw_ref, gb_ref, bd_ref, o_ref, s_ref, dm_ref):
    rows = qk_ref.shape[0]
    kw = qk_ref.shape[1] // 2
    vw = o_ref.shape[1]

    @pl.when((pl.program_id(0) == 0) & (pl.program_id(1) == 0))
    def _():
        rel = (_iota((rows, rows), 0) - _iota((rows, rows), 1)).astype(F32)
        for h in range(D_HEADS):
            dm_ref[h] = jnp.where(rel >= 0, jnp.exp(jnp.maximum(rel, 0.0) * _LOG_GAMMA[h]), 0.0)

    @pl.when(pl.program_id(1) == 0)
    def _():
        s_ref[...] = jnp.zeros(s_ref.shape, F32)

    def rotary(x):
        even = _mod(_iota((rows, LANES), 1), 2) == 0
        parts = []
        for c in range(kw // LANES):
            xc = x[:, c * LANES:(c + 1) * LANES]
            parts.append(jnp.where(even, pltpu.roll(xc, LANES - 1, 1), pltpu.roll(xc, 1, 1)))
        return x * cos_ref[...] + jnp.concatenate(parts, axis=1) * sin_ref[...]

    q = rotary(qk_ref[:, 0:kw])
    k = rotary(qk_ref[:, kw:2 * kw]) * (D_KEY_DIM ** -0.5)
    v = vg_ref[:, 0:vw]
    k_head = _div(_iota((1, kw), 1), D_KEY_DIM)
    lg_k = _per_head(k_head, _LOG_GAMMA)
    lg_v = _per_head(_div(_iota((1, vw), 1), D_VAL_DIM), _LOG_GAMMA)
    t_idx = _iota((rows, 1), 0).astype(F32)

    o_cross = _dot(q * jnp.exp((t_idx + 1.0) * lg_k), s_ref[...])
    inner = []
    for h in range(D_HEADS):
        scores = _dot(jnp.where(k_head == h, q, 0.0), k, NT) * dm_ref[h]
        inner.append(_dot(scores, v[:, h * D_VAL_DIM:(h + 1) * D_VAL_DIM]))
    o = o_cross + jnp.concatenate(inner, axis=1)

    k_dec = k * jnp.exp((rows - 1.0 - t_idx) * lg_k)
    same_head = _div(_iota((kw, vw), 0), D_KEY_DIM) == _div(_iota((kw, vw), 1), D_VAL_DIM)
    s_ref[...] = s_ref[...] * jnp.exp(float(rows) * lg_v) + jnp.where(same_head, _dot(k_dec, v, TN), 0.0)

    inv_n = 1.0 / D_VAL_DIM
    mu = _dot(o, bd_ref[...], pa=2) * inv_n
    oc = o - mu
    var = _dot(oc * oc, bd_ref[...], pa=2) * inv_n
    on = oc * lax.rsqrt(var + EPS) * gw_ref[...] + gb_ref[...]
    o_ref[...] = (_silu(vg_ref[:, vw:2 * vw]) * on).astype(o_ref.dtype)


def _retention(qk, vg, gn_w, gn_b, batch):
    m = qk.shape[0]
    kw = qk.shape[1] // 2
    vw = vg.shape[1] // 2
    rows = MIX_ROWS
    seq = m // batch
    tblocks = seq // rows
    inv = 1.0 / (ROPE_BASE ** jnp.linspace(0.0, 1.0, D_KEY_DIM // 2, dtype=F32))
    ang = jnp.arange(seq).astype(F32)[:, None] * inv[None, :]
    cos = jnp.tile(jnp.repeat(jnp.cos(ang), 2, axis=1), (1, kw // D_KEY_DIM))
    sin = jnp.tile(jnp.stack([-jnp.sin(ang), jnp.sin(ang)], axis=-1).reshape(seq, D_KEY_DIM),
                   (1, kw // D_KEY_DIM))
    lane_head = np.arange(vw) // D_VAL_DIM
    bd = jnp.asarray(lane_head[:, None] == lane_head[None, :], BF16)
    row = lambda p: p.reshape(1, -1).astype(F32)
    row_spec = lambda w: pl.BlockSpec((rows, w), lambda b, i: (b * tblocks + i, 0))
    tab_spec = pl.BlockSpec((rows, kw), lambda b, i: (i, 0))
    return pl.pallas_call(
        _retention_kernel,
        grid=(batch, tblocks),
        in_specs=[row_spec(2 * kw), row_spec(2 * vw), tab_spec, tab_spec, _const_spec((1, vw)),
                  _const_spec((1, vw)), _const_spec(bd.shape)],
        out_specs=row_spec(vw),
        out_shape=jax.ShapeDtypeStruct((m, vw), BF16),
        scratch_shapes=[pltpu.VMEM((kw, vw), F32), pltpu.VMEM((D_HEADS, rows, rows), F32)],
        compiler_params=_params(("arbitrary", "arbitrary")),
        name="retention_mixer",
    )(qk, vg, cos, sin, row(gn_w), row(gn_b), bd)


def _ffn_kernel(x_ref, xh_ref, oa_ref, oah_ref, ob_ref, obh_ref, wo_ref, g_ref, wu_ref, wv_ref, wc_ref, wd_ref,
                *rest, seq_blocks, final):
    if final:
        gf_ref, y_ref, hc_ref, u_ref = rest
    else:
        y_ref, hc_ref, u_ref = rest
    rows = x_ref.shape[0]
    half = oa_ref.shape[1]
    c = pl.program_id(1)

    @pl.when(c == 0)
    def _():
        def mixed(x, oa, ob):
            return (x + jnp.dot(oa, wo_ref[0:half, :], preferred_element_type=F32)
                    + jnp.dot(ob, wo_ref[half:2 * half, :], preferred_element_type=F32))

        x1 = mixed(x_ref[...], oa_ref[...], ob_ref[...])
        x1h = mixed(xh_ref[...], oah_ref[...], obh_ref[...])
        starts_seq = (pl.program_id(0) % seq_blocks) == 0
        y_ref[...] = x1
        hc_ref[0:BF16_ROWS, :] = (_rms(x1h, g_ref[...]) * jnp.where(starts_seq, 0.0, 1.0)).astype(BF16)
        hc_ref[BF16_ROWS:BF16_ROWS + rows, :] = _rms(x1, g_ref[...]).astype(BF16)

    u_ref[...] = jnp.dot(hc_ref[...], wu_ref[...], preferred_element_type=F32)
    v = jnp.dot(hc_ref[BF16_ROWS:BF16_ROWS + rows, :], wv_ref[...], preferred_element_type=F32)
    uc = wc_ref[FFN_CONV - 1:FFN_CONV, :] * u_ref[pl.ds(BF16_ROWS, rows), :]
    for back in range(1, FFN_CONV):
        uc = uc + wc_ref[FFN_CONV - 1 - back:FFN_CONV - back, :] * u_ref[pl.ds(BF16_ROWS - back, rows), :]
    gl = (_silu(uc) * v).astype(BF16)
    y_ref[...] += jnp.dot(gl, wd_ref[...], preferred_element_type=F32)

    if final:
        @pl.when(c == pl.num_programs(1) - 1)
        def _():
            y_ref[...] = _rms(y_ref[...], gf_ref[...])


def _ffn(x, oa, ob, w_out, norm_w, w_up, w_conv, w_down, seq, final_norm=None):
    m, d = x.shape
    half = oa.shape[1]
    ffn = w_down.shape[0]
    tm = FFN_ROWS
    fc = ffn // FFN_SPLIT
    halo = BF16_ROWS
    final = final_norm is not None
    row_spec = lambda w: pl.BlockSpec((tm, w), lambda i, c: (i, 0))
    halo_spec = lambda w: pl.BlockSpec((halo, w), lambda i, c: (jnp.maximum(i * (tm // halo) - 1, 0), 0))
    row = lambda p: p.reshape(1, -1).astype(F32)
    in_specs = [row_spec(d), halo_spec(d), row_spec(half), halo_spec(half), row_spec(half), halo_spec(half),
                _const_spec(w_out.shape), _const_spec((1, d)),
                pl.BlockSpec((d, fc), lambda i, c: (0, c)),
                pl.BlockSpec((d, fc), lambda i, c: (0, FFN_SPLIT + c)),
                pl.BlockSpec((FFN_CONV, fc), lambda i, c: (0, c)),
                pl.BlockSpec((fc, d), lambda i, c: (c, 0))]
    args = [x, x, oa, oa, ob, ob, w_out, row(norm_w), w_up, w_up, w_conv, w_down]
    if final:
        in_specs.append(_const_spec((1, d)))
        args.append(row(final_norm))
    return pl.pallas_call(
        functools.partial(_ffn_kernel, seq_blocks=seq // tm, final=final),
        grid=(m // tm, FFN_SPLIT),
        in_specs=in_specs,
        out_specs=row_spec(d),
        out_shape=jax.ShapeDtypeStruct((m, d), F32),
        scratch_shapes=[pltpu.VMEM((tm + halo, d), BF16), pltpu.VMEM((tm + halo, fc), F32)],
        compiler_params=_params(("parallel", "arbitrary")),
        name="outproj_ffn_final" if final else "outproj_ffn",
    )(*args)


INV_PASSES = 1


def kernel(x, l0_norm1, l0_w_in, l0_a_conv, l0_a_A_log, l0_a_dt_bias, l0_a_norm, l0_b_mu, l0_b_w0, l0_b_w2, l0_b_a0, l0_b_a2, l0_b_g2, l0_b_k_k, l0_b_k_a, l0_b_r_k, l0_b_ln_w, l0_b_ln_b, l0_w_out, l0_norm2, l0_ffn_up, l0_ffn_conv, l0_ffn_down, l1_norm1, l1_w_in, l1_c_conv_w, l1_c_conv_b, l1_c_wa, l1_c_ba, l1_c_wx, l1_c_bx, l1_c_lambda, l1_d_gn_w, l1_d_gn_b, l1_w_out, l1_norm2, l1_ffn_up, l1_ffn_conv, l1_ffn_down, final_norm):
    batch, seq, d = x.shape
    xf = x.reshape(batch * seq, d)
    bf = lambda w: w.astype(BF16)

    a_w = l0_a_conv.shape[1] // 3
    a_heads = l0_a_A_log.shape[0]
    b_w = l0_b_w0.shape[0]
    lora_w, lora_a = l0_b_w2.shape[0], l0_b_a2.shape[0]
    a_in = 4 * a_w + 2 * a_heads
    o_b = a_in + 3 * b_w
    zcols = lambda n: jnp.zeros((d, n), F32)
    w_in0 = jnp.concatenate([
        l0_w_in[:, 0:4 * a_w],
        l0_w_in[:, 4 * a_w:a_in], zcols(LANES - 2 * a_heads),
        l0_w_in[:, a_in:o_b],
        l0_w_in[:, o_b:o_b + lora_w], zcols(B_LORA_PAD - lora_w),
        l0_w_in[:, o_b + lora_w:o_b + lora_w + lora_a], zcols(B_LORA_PAD - lora_a),
        l0_w_in[:, o_b + lora_w + lora_a:]], axis=1)
    zrow = lambda n: jnp.zeros((n,), F32)
    mu0 = jnp.concatenate([
        l0_b_mu[0:3 * b_w + lora_w], zrow(B_LORA_PAD - lora_w),
        l0_b_mu[3 * b_w + lora_w:3 * b_w + lora_w + lora_a], zrow(B_LORA_PAD - lora_a),
        l0_b_mu[3 * b_w + lora_w + lora_a:]])
    pad_rows = lambda w: jnp.concatenate([w, jnp.zeros((B_LORA_PAD - w.shape[0], w.shape[1]), F32)], axis=0)
    b_in = mu0.shape[0]

    qkvz, ba, pb = _norm_proj(xf, l0_norm1, bf(w_in0), (4 * a_w, LANES, b_in), "norm_proj0")
    o_a = _gdn(qkvz, ba, l0_a_conv, l0_a_A_log, l0_a_dt_bias, l0_a_norm, batch, INV_PASSES)
    o_bb = _rwkv(pb, mu0, l0_b_w0, bf(pad_rows(l0_b_w2)), l0_b_a0, bf(pad_rows(l0_b_a2)), bf(l0_b_g2),
                 l0_b_k_k, l0_b_k_a, l0_b_r_k, l0_b_ln_w, l0_b_ln_b, batch, INV_PASSES)
    x1 = _ffn(xf, o_a, o_bb, bf(l0_w_out), l0_norm2, bf(l0_ffn_up), l0_ffn_conv, bf(l0_ffn_down), seq)

    c_w = l1_c_lambda.shape[0]
    d_w = l1_d_gn_w.shape[0]
    qk_w = l1_w_in.shape[1] - 2 * c_w - 2 * d_w
    gx, qk, vg = _norm_proj(x1, l1_norm1, bf(l1_w_in), (2 * c_w, qk_w, 2 * d_w), "norm_proj1")
    o_c = _rglru(gx, l1_c_conv_w, l1_c_conv_b, l1_c_wa, l1_c_ba, l1_c_wx, l1_c_bx, l1_c_lambda, batch)
    o_d = _retention(qk, vg, l1_d_gn_w, l1_d_gn_b, batch)
    y = _ffn(x1, o_c, o_d, bf(l1_w_out), l1_norm2, bf(l1_ffn_up), l1_ffn_conv, bf(l1_ffn_down), seq,
             final_norm=final_norm)
    return y.reshape(batch, seq, d)
```

```python
import functools
import math

import numpy as np
import jax
import jax.numpy as jnp
from jax import lax
from jax.experimental import pallas as pl
from jax.experimental.pallas import tpu as pltpu

F32 = jnp.float32
BF16 = jnp.bfloat16

EPS = 1e-6
CHUNK = 64
A_HEAD_DIM = 128
A_CONV = 4
B_HEAD_DIM = 64
B_LN_EPS = 64e-5
B_LORA_PAD = 128
C_GATE_SCALE = 8.0
C_CONV = 4
D_KEY_DIM = 64
D_VAL_DIM = 128
D_HEADS = 4
ROPE_BASE = 10000.0
FFN_CONV = 3

LANES = 128
SUBLANES = 8
BF16_ROWS = 16
VMEM_LIMIT = 56 * 1024 * 1024

MIX_ROWS = 256
PROJ_ROWS = 256
FFN_ROWS = 512
FFN_SPLIT = 2
RWKV_GROUP = 2

NN = (((1,), (0,)), ((), ()))
NT = (((1,), (1,)), ((), ()))
TN = (((0,), (0,)), ((), ()))


def _pieces(x, n):
    if x.dtype == BF16:
        return [x]
    out, r = [], x
    for i in range(n):
        p = r.astype(BF16)
        out.append(p)
        if i + 1 < n:
            r = r - p.astype(F32)
    return out


def _dot(a, b, dims=NN, pa=1, pb=1):
    ap, bp = _pieces(a, pa), _pieces(b, pb)
    order = max(len(ap), len(bp))
    acc = None
    for i in reversed(range(len(ap))):
        for j in reversed(range(len(bp))):
            if i + j < order:
                t = lax.dot_general(ap[i], bp[j], dims, preferred_element_type=F32)
                acc = t if acc is None else acc + t
    return acc


def _sigmoid(x):
    return 1.0 / (1.0 + jnp.exp(-x))


def _silu(x):
    return x * _sigmoid(x)


def _softplus(x):
    return jnp.maximum(x, 0.0) + jnp.log1p(jnp.exp(-jnp.abs(x)))


def _gelu_tanh(x):
    return 0.5 * x * (1.0 + jnp.tanh(math.sqrt(2.0 / math.pi) * (x + 0.044715 * (x * x * x))))


def _rms(x, g):
    return x * lax.rsqrt(jnp.mean(x * x, axis=-1, keepdims=True) + EPS) * g


def _iota(shape, dim):
    return lax.broadcasted_iota(jnp.int32, shape, dim)


def _div(i, n):
    assert n & (n - 1) == 0
    return i >> (n.bit_length() - 1)


def _mod(i, n):
    assert n & (n - 1) == 0
    return i & (n - 1)


def _pair_masks(n):
    i, j = _iota((n, n), 0), _iota((n, n), 1)
    same = _div(i, CHUNK) == _div(j, CHUNK)
    return same & (_mod(j, CHUNK) < _mod(i, CHUNK)), same & (_mod(j, CHUNK) <= _mod(i, CHUNK))


def _inv_unit_lower(n_mats):
    size = n_mats[0].shape[0]
    eye = (_iota((size, size), 0) == _iota((size, size), 1)).astype(F32)
    ts = [eye + n for n in n_mats]
    ps = list(n_mats)
    for _ in range(int(math.log2(CHUNK)) - 1):
        ps = [_dot(p, p) for p in ps]
        ts = [t + _dot(t, p) for t, p in zip(ts, ps)]
    return ts


def _stack_pair(x, half):
    first = _iota(x.shape, 1) < half
    return jnp.concatenate([jnp.where(first, x, 0.0), jnp.where(first, 0.0, x)], axis=0)


def _shifted_rows(xp_ref, back, rows):
    return xp_ref[pl.ds(SUBLANES - back, rows), :]


def _params(sem):
    return pltpu.CompilerParams(dimension_semantics=sem, vmem_limit_bytes=VMEM_LIMIT)


def _const_spec(shape):
    return pl.BlockSpec(shape, lambda *_: (0,) * len(shape))


def _norm_proj_kernel(x_ref, g_ref, w_ref, *o_refs, splits):
    h = _rms(x_ref[...], g_ref[...]).astype(BF16)
    off = 0
    for o_ref, n in zip(o_refs, splits):
        o_ref[...] = jnp.dot(h, w_ref[:, off:off + n], preferred_element_type=F32)
        off += n


def _norm_proj(x, g, w, splits, name):
    m, d = x.shape
    tm = PROJ_ROWS
    return pl.pallas_call(
        functools.partial(_norm_proj_kernel, splits=splits),
        grid=(m // tm,),
        in_specs=[pl.BlockSpec((tm, d), lambda i: (i, 0)), _const_spec((1, d)), _const_spec(w.shape)],
        out_specs=[pl.BlockSpec((tm, s), lambda i: (i, 0)) for s in splits],
        out_shape=[jax.ShapeDtypeStruct((m, s), F32) for s in splits],
        compiler_params=_params(("parallel",)),
        name=name,
    )(x, g.reshape(1, d), w)


def _gdn_kernel(qkvz_ref, ba_ref, cw_ref, alog_ref, dtb_ref, nw_ref, bd_ref, ex_ref, ltri_ref, lblk_ref,
                o_ref, xp_ref, s_ref, qs_ref, k_ref, kb_ref, vb_ref, kbg_ref, qd_ref, kd_ref, gc_ref,
                gl_ref, oc_ref, us_ref, wq_ref, qkm_ref):
    rows = qkvz_ref.shape[0]
    width = o_ref.shape[1]
    tb = pl.program_id(1)

    @pl.when(tb == 0)
    def _():
        xp_ref[0:SUBLANES, :] = jnp.zeros((SUBLANES, xp_ref.shape[1]), F32)
        s_ref[...] = jnp.zeros(s_ref.shape, F32)

    xp_ref[SUBLANES:SUBLANES + rows, :] = qkvz_ref[:, 0:3 * width]
    acc = cw_ref[A_CONV - 1:A_CONV, :] * _shifted_rows(xp_ref, 0, rows)
    for back in range(1, A_CONV):
        acc = acc + cw_ref[A_CONV - 1 - back:A_CONV - back, :] * _shifted_rows(xp_ref, back, rows)
    xp_ref[0:SUBLANES, :] = xp_ref[rows:rows + SUBLANES, :]
    act = _silu(acc)
    q, k, v = act[:, 0:width], act[:, width:2 * width], act[:, 2 * width:3 * width]

    def l2n(x):
        return x * lax.rsqrt(_dot(x * x, bd_ref[...], pa=2) + EPS)

    ba = ba_ref[...]
    n_heads = width // A_HEAD_DIM
    bg = jnp.where(_iota(ba.shape, 1) < n_heads, _sigmoid(ba),
                   -jnp.exp(alog_ref[...]) * _softplus(ba + dtb_ref[...]))
    cum = _dot(ltri_ref[...], bg, pb=3)
    tot = _dot(lblk_ref[...], bg, pb=3)
    beta_f = _dot(bg, ex_ref[:, 0:width], pa=3)
    gc_f = _dot(cum, ex_ref[:, width:2 * width], pa=3)
    gl_f = _dot(tot, ex_ref[:, width:2 * width], pa=3)

    kn = l2n(k)
    qs = l2n(q) * (A_HEAD_DIM ** -0.5)
    kb = kn * beta_f
    e_gc = jnp.exp(gc_f)
    qs_ref[...] = qs
    k_ref[...] = kn
    kb_ref[...] = kb
    vb_ref[...] = v * beta_f
    kbg_ref[...] = kb * e_gc
    qd_ref[...] = qs * e_gc
    kd_ref[...] = kn * jnp.exp(gl_f - gc_f)
    gc_ref[...] = gc_f
    gl_ref[...] = gl_f

    pair_w = 2 * A_HEAD_DIM
    pair_t = 2 * CHUNK
    n_pairs = width // pair_w
    strict, lower = _pair_masks(pair_t)

    def ld(ref, c, p):
        return ref[c * CHUNK:(c + 1) * CHUNK, p * pair_w:(p + 1) * pair_w]

    def st(ref, c, p):
        return _stack_pair(ld(ref, c, p), A_HEAD_DIM)

    keys = [(c, p) for c in range(rows // CHUNK) for p in range(n_pairs)]
    g = [_dot(jnp.concatenate([st(kb_ref, *k), st(qs_ref, *k)], axis=0), st(k_ref, *k), NT) for k in keys]
    a_ms = []
    for i, k in enumerate(keys):
        gcp = ld(gc_ref, *k)
        gcol = jnp.concatenate([gcp[:, 0:A_HEAD_DIM], gcp[:, A_HEAD_DIM:pair_w]], axis=0)
        dec = jnp.exp(jnp.where(lower, gcol - gcol.T, 0.0))
        a_ms.append(jnp.where(strict, g[i][0:pair_t] * dec, 0.0))
        qkm_ref[i] = jnp.where(lower, g[i][pair_t:2 * pair_t] * dec, 0.0).astype(BF16)
    t_ms = _inv_unit_lower([-a for a in a_ms])
    for i, k in enumerate(keys):
        uw = _dot(t_ms[i], jnp.concatenate([st(vb_ref, *k), st(kbg_ref, *k)], axis=1))
        us_ref[i] = uw[:, 0:pair_w]
        wq_ref[i] = jnp.concatenate([uw[:, pair_w:2 * pair_w], st(qd_ref, *k)], axis=0).astype(BF16)

    for c in range(rows // CHUNK):
        idx = [c * n_pairs + p for p in range(n_pairs)]
        s = [s_ref[p] for p in range(n_pairs)]
        x = [_dot(wq_ref[i], s[p]) for p, i in enumerate(idx)]
        v_new = [us_ref[i] - x[p][0:pair_t] for p, i in enumerate(idx)]
        o_s = [x[p][pair_t:2 * pair_t] + _dot(qkm_ref[i], v_new[p]) for p, i in enumerate(idx)]
        for p in range(n_pairs):
            g_last = jnp.exp(gl_ref[c * CHUNK:c * CHUNK + 1, p * pair_w:(p + 1) * pair_w])
            s_ref[p] = s[p] * g_last + _dot(st(kd_ref, c, p), v_new[p], TN)
            oc_ref[c * CHUNK:(c + 1) * CHUNK, p * pair_w:(p + 1) * pair_w] = (
                o_s[p][0:CHUNK] + o_s[p][CHUNK:pair_t])

    o = oc_ref[...]
    ms = _dot(o * o, bd_ref[...], pa=2) * (1.0 / A_HEAD_DIM)
    z = qkvz_ref[:, 3 * width:4 * width]
    o_ref[...] = (o * lax.rsqrt(ms + EPS) * nw_ref[...] * _silu(z)).astype(o_ref.dtype)


def _gdn(qkvz, ba, conv_w, a_log, dt_bias, norm_w, batch):
    m = qkvz.shape[0]
    width = qkvz.shape[1] // 4
    n_heads = width // A_HEAD_DIM
    rows = MIX_ROWS
    tblocks = m // batch // rows
    pad = lambda p: jnp.zeros((1, LANES), F32).at[0, n_heads:2 * n_heads].set(p)
    lane_head = np.arange(width) // A_HEAD_DIM
    bd = jnp.asarray(lane_head[:, None] == lane_head[None, :], BF16)
    ex = np.zeros((LANES, 2 * width), np.float32)
    for h in range(n_heads):
        ex[h, h * A_HEAD_DIM:(h + 1) * A_HEAD_DIM] = 1.0
        ex[n_heads + h, width + h * A_HEAD_DIM:width + (h + 1) * A_HEAD_DIM] = 1.0
    t = np.arange(rows)
    same = (t[:, None] // CHUNK) == (t[None, :] // CHUNK)
    ltri = jnp.asarray(same & (t[None, :] <= t[:, None]), BF16)
    lblk = jnp.asarray(same, BF16)
    row_spec = lambda w: pl.BlockSpec((rows, w), lambda b, i: (b * tblocks + i, 0))
    act = lambda: pltpu.VMEM((rows, width), F32)
    n_keys = (rows // CHUNK) * (n_heads // 2)
    pair_t, pair_w = 2 * CHUNK, 2 * A_HEAD_DIM
    return pl.pallas_call(
        _gdn_kernel,
        grid=(batch, tblocks),
        in_specs=[row_spec(4 * width), row_spec(LANES), _const_spec(conv_w.shape), _const_spec((1, LANES)),
                  _const_spec((1, LANES)), _const_spec((1, width)), _const_spec(bd.shape),
                  _const_spec(ex.shape), _const_spec(ltri.shape), _const_spec(lblk.shape)],
        out_specs=row_spec(width),
        out_shape=jax.ShapeDtypeStruct((m, width), BF16),
        scratch_shapes=[pltpu.VMEM((rows + SUBLANES, 3 * width), F32),
                        pltpu.VMEM((n_heads // 2, pair_w, pair_w), F32)]
                       + [act() for _ in range(10)]
                       + [pltpu.VMEM((n_keys, pair_t, pair_w), F32), pltpu.VMEM((n_keys, 2 * pair_t, pair_w), BF16),
                          pltpu.VMEM((n_keys, pair_t, pair_t), BF16)],
        compiler_params=_params(("arbitrary", "arbitrary")),
        name="gdn_mixer",
    )(qkvz, ba, conv_w, pad(a_log), pad(dt_bias), jnp.tile(norm_w, n_heads).reshape(1, width), bd,
      jnp.asarray(ex, BF16), ltri, lblk)


def _rwkv_kernel(pb_ref, mu_ref, w0_ref, w2_ref, a0_ref, a2_ref, g2_ref, kk_ref, ka_ref, rk_ref, lnw_ref,
                 lnb_ref, bd_ref, ltri_ref, lblk_ref, o_ref, xp_ref, s_ref, at_ref, rt_ref, bt_ref, kt_ref,
                 v_ref, bdc_ref, kdc_ref, pl_ref, y_ref, bonus_ref, gate_ref, lx_ref, tw_ref, mc_ref):
    rows = pb_ref.shape[0]
    width = o_ref.shape[1]
    tb = pl.program_id(1)

    @pl.when(tb == 0)
    def _():
        xp_ref[0:SUBLANES, :] = jnp.zeros((SUBLANES, xp_ref.shape[1]), F32)
        s_ref[...] = jnp.zeros(s_ref.shape, F32)

    p = pb_ref[...]
    xp_ref[SUBLANES:SUBLANES + rows, :] = p
    prev = _shifted_rows(xp_ref, 1, rows)
    xp_ref[0:SUBLANES, :] = xp_ref[rows:rows + SUBLANES, :]
    m = p + (prev - p) * mu_ref[...]
    r, kr, vr = m[:, 0:width], m[:, width:2 * width], m[:, 2 * width:3 * width]
    o1 = 3 * width
    w_lo = m[:, o1:o1 + B_LORA_PAD]
    a_lo = m[:, o1 + B_LORA_PAD:o1 + 2 * B_LORA_PAD]
    g_lo = m[:, o1 + 2 * B_LORA_PAD:]

    w_log = -_softplus(-(w0_ref[...] + _dot(jnp.tanh(w_lo), w2_ref[...]))) - 0.5
    lw = -jnp.exp(w_log)
    a_lr = _sigmoid(a0_ref[...] + _dot(a_lo, a2_ref[...]))
    gate_ref[...] = _dot(_sigmoid(g_lo), g2_ref[...])

    def gsum(x):
        return _dot(x, bd_ref[...], pa=2)

    kk = kr * kk_ref[...]
    kk = kk * lax.rsqrt(gsum(kk * kk) + EPS)
    k_mod = kr * (1.0 + (a_lr - 1.0) * ka_ref[...])
    a_vec = -kk
    b_vec = kk * a_lr
    bonus_ref[...] = gsum(r * k_mod * rk_ref[...]) * vr

    cum = _dot(ltri_ref[...], lw, pb=3)
    tot = _dot(lblk_ref[...], lw, pb=3)
    e_neg = jnp.exp(-cum)
    e_dec = jnp.exp(tot - cum)
    rt_ref[...] = r * jnp.exp(cum)
    at_ref[...] = a_vec * jnp.exp(cum - lw)
    bt_ref[...] = b_vec * e_neg
    kt_ref[...] = k_mod * e_neg
    bdc_ref[...] = b_vec * e_dec
    kdc_ref[...] = k_mod * e_dec
    pl_ref[...] = jnp.exp(tot)
    v_ref[...] = vr

    pair_w = 2 * B_HEAD_DIM
    pair_t = 2 * CHUNK
    n_pairs = width // pair_w
    n_chunks = rows // CHUNK
    strict, lower = _pair_masks(pair_t)

    def st(ref, c, pr):
        return _stack_pair(ref[c * CHUNK:(c + 1) * CHUNK, pr * pair_w:(pr + 1) * pair_w], B_HEAD_DIM)

    for c0 in range(0, n_chunks, RWKV_GROUP):
        keys = [(c, pr) for c in range(c0, c0 + RWKV_GROUP) for pr in range(n_pairs)]
        a_s = [st(at_ref, *k) for k in keys]
        r_s = [st(rt_ref, *k) for k in keys]
        g = [_dot(jnp.concatenate([a_s[i], r_s[i]], axis=0),
                  jnp.concatenate([st(bt_ref, *k), st(kt_ref, *k)], axis=0), NT)
             for i, k in enumerate(keys)]
        t_ms = _inv_unit_lower([jnp.where(strict, gi[0:pair_t, 0:pair_t], 0.0) for gi in g])
        w1 = [_dot(jnp.where(strict, g[i][0:pair_t, pair_t:2 * pair_t], 0.0), st(v_ref, *k))
              for i, k in enumerate(keys)]
        taw = [_dot(t_ms[i], jnp.concatenate([a_s[i], w1[i]], axis=1)) for i in range(len(keys))]
        for i, (c, pr) in enumerate(keys):
            j = c * n_pairs + pr
            lx_ref[j] = jnp.concatenate([taw[i][:, 0:pair_w], r_s[i]], axis=0).astype(BF16)
            tw_ref[j] = taw[i][:, pair_w:2 * pair_w]
            mc_ref[j] = jnp.where(jnp.concatenate([lower, lower], axis=1), g[i][pair_t:2 * pair_t], 0.0
                                  ).astype(BF16)

    for c in range(n_chunks):
        idx = [c * n_pairs + pr for pr in range(n_pairs)]
        s = [s_ref[pr] for pr in range(n_pairs)]
        x = [_dot(lx_ref[j], s[pr], NT) for pr, j in enumerate(idx)]
        uv = [jnp.concatenate([x[pr][0:pair_t] + tw_ref[j], st(v_ref, c, pr)], axis=0) for pr, j in enumerate(idx)]
        y_s = [x[pr][pair_t:2 * pair_t] + _dot(mc_ref[j], uv[pr]) for pr, j in enumerate(idx)]
        for pr in range(n_pairs):
            p_last = pl_ref[c * CHUNK:c * CHUNK + 1, pr * pair_w:(pr + 1) * pair_w]
            s_ref[pr] = s[pr] * p_last + _dot(
                uv[pr], jnp.concatenate([st(bdc_ref, c, pr), st(kdc_ref, c, pr)], axis=0), TN)
            y_ref[c * CHUNK:(c + 1) * CHUNK, pr * pair_w:(pr + 1) * pair_w] = (
                y_s[pr][0:CHUNK] + y_s[pr][CHUNK:pair_t])

    y = y_ref[...]
    inv_n = 1.0 / B_HEAD_DIM
    mu = gsum(y) * inv_n
    yc = y - mu
    var = gsum(yc * yc) * inv_n
    yn = yc * lax.rsqrt(var + B_LN_EPS) * lnw_ref[...] + lnb_ref[...]
    o_ref[...] = ((yn + bonus_ref[...]) * gate_ref[...]).astype(o_ref.dtype)


def _rwkv(pb, mu, w0, w2, a0, a2, g2, k_k, k_a, r_k, ln_w, ln_b, batch):
    m, in_w = pb.shape
    width = w0.shape[0]
    rows = MIX_ROWS
    tblocks = m // batch // rows
    lane_head = np.arange(width) // B_HEAD_DIM
    bd = jnp.asarray(lane_head[:, None] == lane_head[None, :], BF16)
    t = np.arange(rows)
    same = (t[:, None] // CHUNK) == (t[None, :] // CHUNK)
    ltri = jnp.asarray(same & (t[None, :] <= t[:, None]), BF16)
    lblk = jnp.asarray(same, BF16)
    row = lambda p: p.reshape(1, -1).astype(F32)
    row_spec = lambda w: pl.BlockSpec((rows, w), lambda b, i: (b * tblocks + i, 0))
    vec = _const_spec((1, width))
    act = lambda: pltpu.VMEM((rows, width), F32)
    pair_t, pair_w = 2 * CHUNK, 2 * B_HEAD_DIM
    n_keys = (rows // CHUNK) * (width // pair_w)
    return pl.pallas_call(
        _rwkv_kernel,
        grid=(batch, tblocks),
        in_specs=[row_spec(in_w), _const_spec((1, in_w)), vec, _const_spec(w2.shape), vec,
                  _const_spec(a2.shape), _const_spec(g2.shape), vec, vec, vec, vec, vec,
                  _const_spec(bd.shape), _const_spec(ltri.shape), _const_spec(lblk.shape)],
        out_specs=row_spec(width),
        out_shape=jax.ShapeDtypeStruct((m, width), BF16),
        scratch_shapes=[pltpu.VMEM((rows + SUBLANES, in_w), F32),
                        pltpu.VMEM((width // pair_w, pair_w, pair_w), F32)]
                       + [act() for _ in range(11)]
                       + [pltpu.VMEM((n_keys, 2 * pair_t, pair_w), BF16), pltpu.VMEM((n_keys, pair_t, pair_w), F32),
                          pltpu.VMEM((n_keys, pair_t, 2 * pair_t), BF16)],
        compiler_params=_params(("arbitrary", "arbitrary")),
        name="rwkv_mixer",
    )(pb, row(mu), row(w0), w2, row(a0), a2, g2, row(k_k), row(k_a), row(r_k), row(ln_w), row(ln_b),
      bd, ltri, lblk)


def _rglru_kernel(gx_ref, cw_ref, cb_ref, wa_ref, ba_ref, wx_ref, bx_ref, lam_ref, o_ref, xp_ref, h_ref):
    rows = gx_ref.shape[0]
    width = o_ref.shape[1]
    tb = pl.program_id(1)

    @pl.when(tb == 0)
    def _():
        xp_ref[0:SUBLANES, :] = jnp.zeros((SUBLANES, width), F32)
        h_ref[...] = jnp.zeros(h_ref.shape, F32)

    xp_ref[SUBLANES:SUBLANES + rows, :] = gx_ref[:, width:2 * width]
    xc = cb_ref[...] + cw_ref[C_CONV - 1:C_CONV, :] * _shifted_rows(xp_ref, 0, rows)
    for back in range(1, C_CONV):
        xc = xc + cw_ref[C_CONV - 1 - back:C_CONV - back, :] * _shifted_rows(xp_ref, back, rows)
    xp_ref[0:SUBLANES, :] = xp_ref[rows:rows + SUBLANES, :]

    gate_r = _sigmoid(_dot(xc, wa_ref[...]) + ba_ref[...])
    gate_i = _sigmoid(_dot(xc, wx_ref[...]) + bx_ref[...])
    log_a = -C_GATE_SCALE * gate_r * _softplus(-lam_ref[...])
    a = jnp.exp(log_a)
    d = jnp.sqrt(jnp.tanh(-log_a) * (a * a + 1.0)) * (gate_i * xc)

    t_idx = _iota((rows, width), 0)
    shift = 1
    while shift < rows:
        keep = t_idx >= shift
        a_sh = pltpu.roll(a, shift, 0)
        d_sh = pltpu.roll(d, shift, 0)
        d = jnp.where(keep, a * d_sh + d, d)
        a = jnp.where(keep, a * a_sh, a)
        shift *= 2
    h = a * h_ref[0:1, :] + d
    h_ref[...] = h[rows - SUBLANES:rows, :][SUBLANES - 1:SUBLANES, :] * jnp.ones(h_ref.shape, F32)
    o_ref[...] = (h * _gelu_tanh(gx_ref[:, 0:width])).astype(o_ref.dtype)


def _rglru(gx, conv_w, conv_b, wa, ba, wx, bx, lam, batch):
    m = gx.shape[0]
    width = gx.shape[1] // 2
    rows = MIX_ROWS
    tblocks = m // batch // rows
    row = lambda p: p.reshape(1, -1).astype(F32)
    blockdiag = lambda w: jax.scipy.linalg.block_diag(*[w[i] for i in range(w.shape[0])]).astype(BF16)
    row_spec = lambda w: pl.BlockSpec((rows, w), lambda b, i: (b * tblocks + i, 0))
    vec = _const_spec((1, width))
    mat = _const_spec((width, width))
    return pl.pallas_call(
        _rglru_kernel,
        grid=(batch, tblocks),
        in_specs=[row_spec(2 * width), _const_spec(conv_w.shape), vec, mat, vec, mat, vec, vec],
        out_specs=row_spec(width),
        out_shape=jax.ShapeDtypeStruct((m, width), BF16),
        scratch_shapes=[pltpu.VMEM((rows + SUBLANES, width), F32), pltpu.VMEM((SUBLANES, width), F32)],
        compiler_params=_params(("arbitrary", "arbitrary")),
        name="rglru_mixer",
    )(gx, conv_w, row(conv_b), blockdiag(wa), row(ba), blockdiag(wx), row(bx), row(lam))


_LOG_GAMMA = tuple(math.log(1.0 - 2.0 ** (-5.0 - h)) for h in range(D_HEADS))


def _per_head(lane_head, values):
    out = jnp.full(lane_head.shape, values[-1], F32)
    for h in range(len(values) - 2, -1, -1):
        out = jnp.where(lane_head == h, values[h], out)
    return out


def _retention_kernel(qk_ref, vg_ref, cos_ref, sin_ref, gw_ref, gb_ref, bd_ref, o_ref, s_ref, dm_ref):
    rows = qk_ref.shape[0]
    kw = qk_ref.shape[1] // 2
    vw = o_ref.shape[1]

    @pl.when((pl.program_id(0) == 0) & (pl.program_id(1) == 0))
    def _():
        rel = (_iota((rows, rows), 0) - _iota((rows, rows), 1)).astype(F32)
        for h in range(D_HEADS):
            dm_ref[h] = jnp.where(rel >= 0, jnp.exp(jnp.maximum(rel, 0.0) * _LOG_GAMMA[h]), 0.0)

    @pl.when(pl.program_id(1) == 0)
    def _():
        s_ref[...] = jnp.zeros(s_ref.shape, F32)

    def rotary(x):
        even = _mod(_iota((rows, LANES), 1), 2) == 0
        parts = []
        for c in range(kw // LANES):
            xc = x[:, c * LANES:(c + 1) * LANES]
            parts.append(jnp.where(even, pltpu.roll(xc, LANES - 1, 1), pltpu.roll(xc, 1, 1)))
        return x * cos_ref[...] + jnp.concatenate(parts, axis=1) * sin_ref[...]

    q = rotary(qk_ref[:, 0:kw])
    k = rotary(qk_ref[:, kw:2 * kw]) * (D_KEY_DIM ** -0.5)
    v = vg_ref[:, 0:vw]
    k_head = _div(_iota((1, kw), 1), D_KEY_DIM)
    lg_k = _per_head(k_head, _LOG_GAMMA)
    lg_v = _per_head(_div(_iota((1, vw), 1), D_VAL_DIM), _LOG_GAMMA)
    t_idx = _iota((rows, 1), 0).astype(F32)

    o_cross = _dot(q * jnp.exp((t_idx + 1.0) * lg_k), s_ref[...])
    inner = []
    for h in range(D_HEADS):
        scores = _dot(jnp.where(k_head == h, q, 0.0), k, NT) * dm_ref[h]
        inner.append(_dot(scores, v[:, h * D_VAL_DIM:(h + 1) * D_VAL_DIM]))
    o = o_cross + jnp.concatenate(inner, axis=1)

    k_dec = k * jnp.exp((rows - 1.0 - t_idx) * lg_k)
    same_head = _div(_iota((kw, vw), 0), D_KEY_DIM) == _div(_iota((kw, vw), 1), D_VAL_DIM)
    s_ref[...] = s_ref[...] * jnp.exp(float(rows) * lg_v) + jnp.where(same_head, _dot(k_dec, v, TN), 0.0)

    inv_n = 1.0 / D_VAL_DIM
    mu = _dot(o, bd_ref[...], pa=2) * inv_n
    oc = o - mu
    var = _dot(oc * oc, bd_ref[...], pa=2) * inv_n
    on = oc * lax.rsqrt(var + EPS) * gw_ref[...] + gb_ref[...]
    o_ref[...] = (_silu(vg_ref[:, vw:2 * vw]) * on).astype(o_ref.dtype)


def _retention(qk, vg, gn_w, gn_b, batch):
    m = qk.shape[0]
    kw = qk.shape[1] // 2
    vw = vg.shape[1] // 2
    rows = MIX_ROWS
    seq = m // batch
    tblocks = seq // rows
    inv = 1.0 / (ROPE_BASE ** jnp.linspace(0.0, 1.0, D_KEY_DIM // 2, dtype=F32))
    ang = jnp.arange(seq).astype(F32)[:, None] * inv[None, :]
    cos = jnp.tile(jnp.repeat(jnp.cos(ang), 2, axis=1), (1, kw // D_KEY_DIM))
    sin = jnp.tile(jnp.stack([-jnp.sin(ang), jnp.sin(ang)], axis=-1).reshape(seq, D_KEY_DIM),
                   (1, kw // D_KEY_DIM))
    lane_head = np.arange(vw) // D_VAL_DIM
    bd = jnp.asarray(lane_head[:, None] == lane_head[None, :], BF16)
    row = lambda p: p.reshape(1, -1).astype(F32)
    row_spec = lambda w: pl.BlockSpec((rows, w), lambda b, i: (b * tblocks + i, 0))
    tab_spec = pl.BlockSpec((rows, kw), lambda b, i: (i, 0))
    return pl.pallas_call(
        _retention_kernel,
        grid=(batch, tblocks),
        in_specs=[row_spec(2 * kw), row_spec(2 * vw), tab_spec, tab_spec, _const_spec((1, vw)),
                  _const_spec((1, vw)), _const_spec(bd.shape)],
        out_specs=row_spec(vw),
        out_shape=jax.ShapeDtypeStruct((m, vw), BF16),
        scratch_shapes=[pltpu.VMEM((kw, vw), F32), pltpu.VMEM((D_HEADS, rows, rows), F32)],
        compiler_params=_params(("arbitrary", "arbitrary")),
        name="retention_mixer",
    )(qk, vg, cos, sin, row(gn_w), row(gn_b), bd)


def _ffn_kernel(x_ref, xh_ref, oa_ref, oah_ref, ob_ref, obh_ref, wo_ref, g_ref, wu_ref, wv_ref, wc_ref, wd_ref,
                *rest, seq_blocks, final):
    if final:
        gf_ref, y_ref, hc_ref, u_ref = rest
    else:
        y_ref, hc_ref, u_ref = rest
    rows = x_ref.shape[0]
    half = oa_ref.shape[1]
    c = pl.program_id(1)

    @pl.when(c == 0)
    def _():
        def mixed(x, oa, ob):
            return (x + jnp.dot(oa, wo_ref[0:half, :], preferred_element_type=F32)
                    + jnp.dot(ob, wo_ref[half:2 * half, :], preferred_element_type=F32))

        x1 = mixed(x_ref[...], oa_ref[...], ob_ref[...])
        x1h = mixed(xh_ref[...], oah_ref[...], obh_ref[...])
        starts_seq = (pl.program_id(0) % seq_blocks) == 0
        y_ref[...] = x1
        hc_ref[0:BF16_ROWS, :] = (_rms(x1h, g_ref[...]) * jnp.where(starts_seq, 0.0, 1.0)).astype(BF16)
        hc_ref[BF16_ROWS:BF16_ROWS + rows, :] = _rms(x1, g_ref[...]).astype(BF16)

    u_ref[...] = jnp.dot(hc_ref[...], wu_ref[...], preferred_element_type=F32)
    v = jnp.dot(hc_ref[BF16_ROWS:BF16_ROWS + rows, :], wv_ref[...], preferred_element_type=F32)
    uc = wc_ref[FFN_CONV - 1:FFN_CONV, :] * u_ref[pl.ds(BF16_ROWS, rows), :]
    for back in range(1, FFN_CONV):
        uc = uc + wc_ref[FFN_CONV - 1 - back:FFN_CONV - back, :] * u_ref[pl.ds(BF16_ROWS - back, rows), :]
    gl = (_silu(uc) * v).astype(BF16)
    y_ref[...] += jnp.dot(gl, wd_ref[...], preferred_element_type=F32)

    if final:
        @pl.when(c == pl.num_programs(1) - 1)
        def _():
            y_ref[...] = _rms(y_ref[...], gf_ref[...])


def _ffn(x, oa, ob, w_out, norm_w, w_up, w_conv, w_down, seq, final_norm=None):
    m, d = x.shape
    half = oa.shape[1]
    ffn = w_down.shape[0]
    tm = FFN_ROWS
    fc = ffn // FFN_SPLIT
    halo = BF16_ROWS
    final = final_norm is not None
    row_spec = lambda w: pl.BlockSpec((tm, w), lambda i, c: (i, 0))
    halo_spec = lambda w: pl.BlockSpec((halo, w), lambda i, c: (jnp.maximum(i * (tm // halo) - 1, 0), 0))
    row = lambda p: p.reshape(1, -1).astype(F32)
    in_specs = [row_spec(d), halo_spec(d), row_spec(half), halo_spec(half), row_spec(half), halo_spec(half),
                _const_spec(w_out.shape), _const_spec((1, d)),
                pl.BlockSpec((d, fc), lambda i, c: (0, c)),
                pl.BlockSpec((d, fc), lambda i, c: (0, FFN_SPLIT + c)),
                pl.BlockSpec((FFN_CONV, fc), lambda i, c: (0, c)),
                pl.BlockSpec((fc, d), lambda i, c: (c, 0))]
    args = [x, x, oa, oa, ob, ob, w_out, row(norm_w), w_up, w_up, w_conv, w_down]
    if final:
        in_specs.append(_const_spec((1, d)))
        args.append(row(final_norm))
    return pl.pallas_call(
        functools.partial(_ffn_kernel, seq_blocks=seq // tm, final=final),
        grid=(m // tm, FFN_SPLIT),
        in_specs=in_specs,
        out_specs=row_spec(d),
        out_shape=jax.ShapeDtypeStruct((m, d), F32),
        scratch_shapes=[pltpu.VMEM((tm + halo, d), BF16), pltpu.VMEM((tm + halo, fc), F32)],
        compiler_params=_params(("parallel", "arbitrary")),
        name="outproj_ffn_final" if final else "outproj_ffn",
    )(*args)


def kernel(x, l0_norm1, l0_w_in, l0_a_conv, l0_a_A_log, l0_a_dt_bias, l0_a_norm, l0_b_mu, l0_b_w0, l0_b_w2, l0_b_a0, l0_b_a2, l0_b_g2, l0_b_k_k, l0_b_k_a, l0_b_r_k, l0_b_ln_w, l0_b_ln_b, l0_w_out, l0_norm2, l0_ffn_up, l0_ffn_conv, l0_ffn_down, l1_norm1, l1_w_in, l1_c_conv_w, l1_c_conv_b, l1_c_wa, l1_c_ba, l1_c_wx, l1_c_bx, l1_c_lambda, l1_d_gn_w, l1_d_gn_b, l1_w_out, l1_norm2, l1_ffn_up, l1_ffn_conv, l1_ffn_down, final_norm):
    batch, seq, d = x.shape
    xf = x.reshape(batch * seq, d)
    bf = lambda w: w.astype(BF16)

    a_w = l0_a_conv.shape[1] // 3
    a_heads = l0_a_A_log.shape[0]
    b_w = l0_b_w0.shape[0]
    lora_w, lora_a = l0_b_w2.shape[0], l0_b_a2.shape[0]
    a_in = 4 * a_w + 2 * a_heads
    o_b = a_in + 3 * b_w
    zcols = lambda n: jnp.zeros((d, n), F32)
    w_in0 = jnp.concatenate([
        l0_w_in[:, 0:4 * a_w],
        l0_w_in[:, 4 * a_w:a_in], zcols(LANES - 2 * a_heads),
        l0_w_in[:, a_in:o_b],
        l0_w_in[:, o_b:o_b + lora_w], zcols(B_LORA_PAD - lora_w),
        l0_w_in[:, o_b + lora_w:o_b + lora_w + lora_a], zcols(B_LORA_PAD - lora_a),
        l0_w_in[:, o_b + lora_w + lora_a:]], axis=1)
    zrow = lambda n: jnp.zeros((n,), F32)
    mu0 = jnp.concatenate([
        l0_b_mu[0:3 * b_w + lora_w], zrow(B_LORA_PAD - lora_w),
        l0_b_mu[3 * b_w + lora_w:3 * b_w + lora_w + lora_a], zrow(B_LORA_PAD - lora_a),
        l0_b_mu[3 * b_w + lora_w + lora_a:]])
    pad_rows = lambda w: jnp.concatenate([w, jnp.zeros((B_LORA_PAD - w.shape[0], w.shape[1]), F32)], axis=0)
    b_in = mu0.shape[0]

    qkvz, ba, pb = _norm_proj(xf, l0_norm1, bf(w_in0), (4 * a_w, LANES, b_in), "norm_proj0")
    o_a = _gdn(qkvz, ba, l0_a_conv, l0_a_A_log, l0_a_dt_bias, l0_a_norm, batch)
    o_bb = _rwkv(pb, mu0, l0_b_w0, bf(pad_rows(l0_b_w2)), l0_b_a0, bf(pad_rows(l0_b_a2)), bf(l0_b_g2),
                 l0_b_k_k, l0_b_k_a, l0_b_r_k, l0_b_ln_w, l0_b_ln_b, batch)
    x1 = _ffn(xf, o_a, o_bb, bf(l0_w_out), l0_norm2, bf(l0_ffn_up), l0_ffn_conv, bf(l0_ffn_down), seq)

    c_w = l1_c_lambda.shape[0]
    d_w = l1_d_gn_w.shape[0]
    qk_w = l1_w_in.shape[1] - 2 * c_w - 2 * d_w
    gx, qk, vg = _norm_proj(x1, l1_norm1, bf(l1_w_in), (2 * c_w, qk_w, 2 * d_w), "norm_proj1")
    o_c = _rglru(gx, l1_c_conv_w, l1_c_conv_b, l1_c_wa, l1_c_ba, l1_c_wx, l1_c_bx, l1_c_lambda, batch)
    o_d = _retention(qk, vg, l1_d_gn_w, l1_d_gn_b, batch)
    y = _ffn(x1, o_c, o_d, bf(l1_w_out), l1_norm2, bf(l1_ffn_up), l1_ffn_conv, bf(l1_ffn_down), seq,
             final_norm=final_norm)
    return y.reshape(batch, seq, d)
```

```python
import functools
import math

import numpy as np
import jax
import jax.numpy as jnp
from jax import lax
from jax.experimental import pallas as pl
from jax.experimental.pallas import tpu as pltpu

F32 = jnp.float32
BF16 = jnp.bfloat16

EPS = 1e-6
CHUNK = 64
A_HEAD_DIM = 128
A_CONV = 4
B_HEAD_DIM = 64
B_LN_EPS = 64e-5
B_LORA_PAD = 128
C_GATE_SCALE = 8.0
C_CONV = 4
D_KEY_DIM = 64
D_VAL_DIM = 128
D_HEADS = 4
ROPE_BASE = 10000.0
FFN_CONV = 3

LANES = 128
SUBLANES = 8
BF16_ROWS = 16
VMEM_LIMIT = 56 * 1024 * 1024

MIX_ROWS = 256
PROJ_ROWS = 256
FFN_ROWS = 512
FFN_SPLIT = 1
RWKV_GROUP = 2

NN = (((1,), (0,)), ((), ()))
NT = (((1,), (1,)), ((), ()))
TN = (((0,), (0,)), ((), ()))


def _pieces(x, n):
    if x.dtype == BF16:
        return [x]
    out, r = [], x
    for i in range(n):
        p = r.astype(BF16)
        out.append(p)
        if i + 1 < n:
            r = r - p.astype(F32)
    return out


def _dot(a, b, dims=NN, pa=1, pb=1):
    ap, bp = _pieces(a, pa), _pieces(b, pb)
    order = max(len(ap), len(bp))
    acc = None
    for i in reversed(range(len(ap))):
        for j in reversed(range(len(bp))):
            if i + j < order:
                t = lax.dot_general(ap[i], bp[j], dims, preferred_element_type=F32)
                acc = t if acc is None else acc + t
    return acc


def _sigmoid(x):
    return 1.0 / (1.0 + jnp.exp(-x))


def _silu(x):
    return x * _sigmoid(x)


def _softplus(x):
    return jnp.maximum(x, 0.0) + jnp.log1p(jnp.exp(-jnp.abs(x)))


def _gelu_tanh(x):
    return 0.5 * x * (1.0 + jnp.tanh(math.sqrt(2.0 / math.pi) * (x + 0.044715 * (x * x * x))))


def _rms(x, g):
    return x * lax.rsqrt(jnp.mean(x * x, axis=-1, keepdims=True) + EPS) * g


def _iota(shape, dim):
    return lax.broadcasted_iota(jnp.int32, shape, dim)


def _div(i, n):
    assert n & (n - 1) == 0
    return i >> (n.bit_length() - 1)


def _mod(i, n):
    assert n & (n - 1) == 0
    return i & (n - 1)


def _pair_masks():
    t, s = _iota((CHUNK, 2 * CHUNK), 0), _mod(_iota((CHUNK, 2 * CHUNK), 1), CHUNK)
    return s < t, s <= t


def _stack_pair(x):
    first = _iota(x.shape, 1) < x.shape[1] // 2
    return jnp.concatenate([jnp.where(first, x, 0.0), jnp.where(first, 0.0, x)], axis=0)


def _inv_unit_lower(n_mats):
    shape = n_mats[0].shape
    eye = (_mod(_iota(shape, 1), CHUNK) == _iota(shape, 0)).astype(F32)
    ts = [eye + n for n in n_mats]
    ps = [_dot(n, _stack_pair(n)) for n in n_mats]
    levels = int(math.log2(CHUNK)) - 1
    for level in range(levels):
        last = level == levels - 1
        lhs = ts if last else [jnp.concatenate([t, p], axis=0) for t, p in zip(ts, ps)]
        prod = [_dot(l, _stack_pair(p)) for l, p in zip(lhs, ps)]
        ts = [t + pr[0:CHUNK] for t, pr in zip(ts, prod)]
        if not last:
            ps = [pr[CHUNK:2 * CHUNK] for pr in prod]
    return ts


def _head_sums(x, dim):
    parts = [jnp.broadcast_to(jnp.sum(x[:, h * dim:(h + 1) * dim], axis=-1, keepdims=True), (x.shape[0], dim))
             for h in range(x.shape[1] // dim)]
    return jnp.concatenate(parts, axis=1)


def _head_expand(cols, first, heads, dim):
    return jnp.concatenate([jnp.broadcast_to(cols[:, first + h:first + h + 1], (cols.shape[0], dim))
                            for h in range(heads)], axis=1)


def _shifted_rows(xp_ref, back, rows):
    return xp_ref[pl.ds(SUBLANES - back, rows), :]


def _params(sem):
    return pltpu.CompilerParams(dimension_semantics=sem, vmem_limit_bytes=VMEM_LIMIT)


def _const_spec(shape):
    return pl.BlockSpec(shape, lambda *_: (0,) * len(shape))


def _chunk_sum_matrix(rows):
    t = np.arange(rows)
    same = (t[:, None] // CHUNK) == (t[None, :] // CHUNK)
    return jnp.asarray(np.concatenate([same & (t[None, :] <= t[:, None]), same], axis=0), BF16)


def _norm_proj_kernel(x_ref, g_ref, w_ref, *o_refs, splits):
    h = _rms(x_ref[...], g_ref[...]).astype(BF16)
    off = 0
    for o_ref, n in zip(o_refs, splits):
        o_ref[...] = jnp.dot(h, w_ref[:, off:off + n], preferred_element_type=F32)
        off += n


def _norm_proj(x, g, w, splits, name):
    m, d = x.shape
    tm = PROJ_ROWS
    return pl.pallas_call(
        functools.partial(_norm_proj_kernel, splits=splits),
        grid=(m // tm,),
        in_specs=[pl.BlockSpec((tm, d), lambda i: (i, 0)), _const_spec((1, d)), _const_spec(w.shape)],
        out_specs=[pl.BlockSpec((tm, s), lambda i: (i, 0)) for s in splits],
        out_shape=[jax.ShapeDtypeStruct((m, s), F32) for s in splits],
        compiler_params=_params(("parallel",)),
        name=name,
    )(x, g.reshape(1, d), w)


def _gdn_kernel(qkvz_ref, ba_ref, cw_ref, alog_ref, dtb_ref, nw_ref, lcat_ref,
                o_ref, xp_ref, s_ref, qs_ref, k_ref, kb_ref, vb_ref, kbg_ref, qd_ref, kd_ref, gc_ref,
                gl_ref, oc_ref, us_ref, wq_ref, qkm_ref):
    rows = qkvz_ref.shape[0]
    width = o_ref.shape[1]
    tb = pl.program_id(1)

    @pl.when(tb == 0)
    def _():
        xp_ref[0:SUBLANES, :] = jnp.zeros((SUBLANES, xp_ref.shape[1]), F32)
        s_ref[...] = jnp.zeros(s_ref.shape, F32)

    xp_ref[SUBLANES:SUBLANES + rows, :] = qkvz_ref[:, 0:3 * width]
    acc = cw_ref[A_CONV - 1:A_CONV, :] * _shifted_rows(xp_ref, 0, rows)
    for back in range(1, A_CONV):
        acc = acc + cw_ref[A_CONV - 1 - back:A_CONV - back, :] * _shifted_rows(xp_ref, back, rows)
    xp_ref[0:SUBLANES, :] = xp_ref[rows:rows + SUBLANES, :]
    act = _silu(acc)
    q, k, v = act[:, 0:width], act[:, width:2 * width], act[:, 2 * width:3 * width]

    def l2n(x):
        return x * lax.rsqrt(_head_sums(x * x, A_HEAD_DIM) + EPS)

    ba = ba_ref[...]
    n_heads = width // A_HEAD_DIM
    bg = jnp.where(_iota(ba.shape, 1) < n_heads, _sigmoid(ba),
                   -jnp.exp(alog_ref[...]) * _softplus(ba + dtb_ref[...]))
    ct = _dot(lcat_ref[...], bg, pb=3)
    beta_f = _head_expand(bg, 0, n_heads, A_HEAD_DIM)
    gc_f = _head_expand(ct[0:rows], n_heads, n_heads, A_HEAD_DIM)
    gl_f = _head_expand(ct[rows:2 * rows], n_heads, n_heads, A_HEAD_DIM)

    kn = l2n(k)
    qs = l2n(q) * (A_HEAD_DIM ** -0.5)
    kb = kn * beta_f
    e_gc = jnp.exp(gc_f)
    qs_ref[...] = qs
    k_ref[...] = kn
    kb_ref[...] = kb
    vb_ref[...] = v * beta_f
    kbg_ref[...] = kb * e_gc
    qd_ref[...] = qs * e_gc
    kd_ref[...] = kn * jnp.exp(gl_f - gc_f)
    gc_ref[...] = gc_f
    gl_ref[...] = gl_f

    pair_w = 2 * A_HEAD_DIM
    assert A_HEAD_DIM == 2 * CHUNK
    n_pairs = width // pair_w
    n_chunks = rows // CHUNK
    strict, lower = _pair_masks()
    first_head = _iota((CHUNK, 2 * CHUNK), 1) < CHUNK
    same_head = (_iota((pair_w, pair_w), 0) < A_HEAD_DIM) == (_iota((pair_w, pair_w), 1) < A_HEAD_DIM)

    def ld(ref, c, p):
        return ref[c * CHUNK:(c + 1) * CHUNK, p * pair_w:(p + 1) * pair_w]

    keys = [(c, p) for c in range(n_chunks) for p in range(n_pairs)]
    g = [_dot(jnp.concatenate([ld(kb_ref, *k), ld(qs_ref, *k)], axis=0), _stack_pair(ld(k_ref, *k)), NT)
         for k in keys]
    n_ms = []
    for i, k in enumerate(keys):
        gcp = ld(gc_ref, *k)
        g_t = jnp.where(first_head, gcp[:, 0:A_HEAD_DIM], gcp[:, A_HEAD_DIM:pair_w])
        g_s = jnp.concatenate([gcp[:, 0:A_HEAD_DIM], gcp[:, A_HEAD_DIM:pair_w]], axis=0).T[0:CHUNK]
        dec = jnp.exp(jnp.where(lower, g_t - g_s, 0.0))
        n_ms.append(jnp.where(strict, -(g[i][0:CHUNK] * dec), 0.0))
        qkm_ref[i] = jnp.where(lower, g[i][CHUNK:2 * CHUNK] * dec, 0.0).astype(BF16)
    t_ms = _inv_unit_lower(n_ms)
    for i, k in enumerate(keys):
        uw = _dot(t_ms[i], jnp.concatenate([_stack_pair(ld(vb_ref, *k)), _stack_pair(ld(kbg_ref, *k))], axis=1))
        us_ref[i] = uw[:, 0:pair_w]
        wq_ref[i] = jnp.concatenate([uw[:, pair_w:2 * pair_w], ld(qd_ref, *k)], axis=0).astype(BF16)

    for c in range(n_chunks):
        idx = [c * n_pairs + p for p in range(n_pairs)]
        s = [s_ref[p] for p in range(n_pairs)]
        x = [_dot(wq_ref[i], s[p]) for p, i in enumerate(idx)]
        v_new = [us_ref[i] - x[p][0:CHUNK] for p, i in enumerate(idx)]
        o_c = [x[p][CHUNK:2 * CHUNK] + _dot(qkm_ref[i], _stack_pair(v_new[p])) for p, i in enumerate(idx)]
        for p in range(n_pairs):
            g_last = jnp.exp(gl_ref[c * CHUNK:c * CHUNK + 1, p * pair_w:(p + 1) * pair_w])
            s_ref[p] = s[p] * g_last + jnp.where(same_head, _dot(ld(kd_ref, c, p), v_new[p], TN), 0.0)
            oc_ref[c * CHUNK:(c + 1) * CHUNK, p * pair_w:(p + 1) * pair_w] = o_c[p]

    o = oc_ref[...]
    ms = _head_sums(o * o, A_HEAD_DIM) * (1.0 / A_HEAD_DIM)
    z = qkvz_ref[:, 3 * width:4 * width]
    o_ref[...] = (o * lax.rsqrt(ms + EPS) * nw_ref[...] * _silu(z)).astype(o_ref.dtype)


def _gdn(qkvz, ba, conv_w, a_log, dt_bias, norm_w, batch):
    m = qkvz.shape[0]
    width = qkvz.shape[1] // 4
    n_heads = width // A_HEAD_DIM
    rows = MIX_ROWS
    tblocks = m // batch // rows
    pad = lambda p: jnp.zeros((1, LANES), F32).at[0, n_heads:2 * n_heads].set(p)
    lcat = _chunk_sum_matrix(rows)
    row_spec = lambda w: pl.BlockSpec((rows, w), lambda b, i: (b * tblocks + i, 0))
    act = lambda: pltpu.VMEM((rows, width), F32)
    pair_w = 2 * A_HEAD_DIM
    n_keys = (rows // CHUNK) * (width // pair_w)
    return pl.pallas_call(
        _gdn_kernel,
        grid=(batch, tblocks),
        in_specs=[row_spec(4 * width), row_spec(LANES), _const_spec(conv_w.shape), _const_spec((1, LANES)),
                  _const_spec((1, LANES)), _const_spec((1, width)), _const_spec(lcat.shape)],
        out_specs=row_spec(width),
        out_shape=jax.ShapeDtypeStruct((m, width), BF16),
        scratch_shapes=[pltpu.VMEM((rows + SUBLANES, 3 * width), F32),
                        pltpu.VMEM((width // pair_w, pair_w, pair_w), F32)]
                       + [act() for _ in range(10)]
                       + [pltpu.VMEM((n_keys, CHUNK, pair_w), F32), pltpu.VMEM((n_keys, 2 * CHUNK, pair_w), BF16),
                          pltpu.VMEM((n_keys, CHUNK, 2 * CHUNK), BF16)],
        compiler_params=_params(("arbitrary", "arbitrary")),
        name="gdn_mixer",
    )(qkvz, ba, conv_w, pad(a_log), pad(dt_bias), jnp.tile(norm_w, n_heads).reshape(1, width), lcat)


def _rwkv_kernel(pb_ref, mu_ref, w0_ref, w2_ref, a0_ref, a2_ref, g2_ref, kk_ref, ka_ref, rk_ref, lnw_ref,
                 lnb_ref, bd_ref, lcat_ref, o_ref, xp_ref, s_ref, at_ref, rt_ref, bt_ref, kt_ref,
                 v_ref, bdc_ref, kdc_ref, pl_ref, y_ref, bonus_ref, gate_ref, lx_ref, tw_ref, mc_ref):
    rows = pb_ref.shape[0]
    width = o_ref.shape[1]
    tb = pl.program_id(1)

    @pl.when(tb == 0)
    def _():
        xp_ref[0:SUBLANES, :] = jnp.zeros((SUBLANES, xp_ref.shape[1]), F32)
        s_ref[...] = jnp.zeros(s_ref.shape, F32)

    p = pb_ref[...]
    xp_ref[SUBLANES:SUBLANES + rows, :] = p
    prev = _shifted_rows(xp_ref, 1, rows)
    xp_ref[0:SUBLANES, :] = xp_ref[rows:rows + SUBLANES, :]
    m = p + (prev - p) * mu_ref[...]
    r, kr, vr = m[:, 0:width], m[:, width:2 * width], m[:, 2 * width:3 * width]
    o1 = 3 * width
    w_lo = m[:, o1:o1 + B_LORA_PAD]
    a_lo = m[:, o1 + B_LORA_PAD:o1 + 2 * B_LORA_PAD]
    g_lo = m[:, o1 + 2 * B_LORA_PAD:]

    w_log = -_softplus(-(w0_ref[...] + _dot(jnp.tanh(w_lo), w2_ref[...]))) - 0.5
    lw = -jnp.exp(w_log)
    a_lr = _sigmoid(a0_ref[...] + _dot(a_lo, a2_ref[...]))
    gate_ref[...] = _dot(_sigmoid(g_lo), g2_ref[...])

    def gsum(x):
        return _dot(x, bd_ref[...])

    kk = kr * kk_ref[...]
    k_mod = kr * (1.0 + (a_lr - 1.0) * ka_ref[...])
    sums = gsum(jnp.concatenate([kk * kk, r * k_mod * rk_ref[...]], axis=0))
    kk = kk * lax.rsqrt(sums[0:rows] + EPS)
    a_vec = -kk
    b_vec = kk * a_lr
    bonus_ref[...] = sums[rows:2 * rows] * vr

    ct = _dot(lcat_ref[...], lw, pb=2)
    cum, tot = ct[0:rows], ct[rows:2 * rows]
    e_neg = jnp.exp(-cum)
    e_dec = jnp.exp(tot - cum)
    rt_ref[...] = r * jnp.exp(cum)
    at_ref[...] = a_vec * jnp.exp(cum - lw)
    bt_ref[...] = b_vec * e_neg
    kt_ref[...] = k_mod * e_neg
    bdc_ref[...] = b_vec * e_dec
    kdc_ref[...] = k_mod * e_dec
    pl_ref[...] = jnp.exp(tot)
    v_ref[...] = vr

    pair_w = 2 * B_HEAD_DIM
    n_pairs = width // pair_w
    n_chunks = rows // CHUNK
    strict, lower = _pair_masks()
    lower2 = jnp.concatenate([lower, lower], axis=1)
    same_head = (_iota((pair_w, pair_w), 0) < B_HEAD_DIM) == (_iota((pair_w, pair_w), 1) < B_HEAD_DIM)

    def ld(ref, c, pr):
        return ref[c * CHUNK:(c + 1) * CHUNK, pr * pair_w:(pr + 1) * pair_w]

    for c0 in range(0, n_chunks, RWKV_GROUP):
        keys = [(c, pr) for c in range(c0, c0 + RWKV_GROUP) for pr in range(n_pairs)]
        g = [_dot(jnp.concatenate([ld(at_ref, *k), ld(rt_ref, *k)], axis=0),
                  jnp.concatenate([_stack_pair(ld(bt_ref, *k)), _stack_pair(ld(kt_ref, *k))], axis=0), NT)
             for k in keys]
        t_ms = _inv_unit_lower([jnp.where(strict, gi[0:CHUNK, 0:2 * CHUNK], 0.0) for gi in g])
        w1 = [_dot(jnp.where(strict, g[i][0:CHUNK, 2 * CHUNK:4 * CHUNK], 0.0), _stack_pair(ld(v_ref, *k)))
              for i, k in enumerate(keys)]
        taw = [_dot(t_ms[i], jnp.concatenate([_stack_pair(ld(at_ref, *k)), _stack_pair(w1[i])], axis=1))
               for i, k in enumerate(keys)]
        for i, (c, pr) in enumerate(keys):
            j = c * n_pairs + pr
            lx_ref[j] = jnp.concatenate([taw[i][:, 0:pair_w], ld(rt_ref, c, pr)], axis=0).astype(BF16)
            tw_ref[j] = taw[i][:, pair_w:2 * pair_w]
            mc_ref[j] = jnp.where(lower2, g[i][CHUNK:2 * CHUNK], 0.0).astype(BF16)

    for c in range(n_chunks):
        idx = [c * n_pairs + pr for pr in range(n_pairs)]
        s = [s_ref[pr] for pr in range(n_pairs)]
        x = [_dot(lx_ref[j], s[pr], NT) for pr, j in enumerate(idx)]
        u = [x[pr][0:CHUNK] + tw_ref[j] for pr, j in enumerate(idx)]
        y_c = [x[pr][CHUNK:2 * CHUNK]
               + _dot(mc_ref[j], jnp.concatenate([_stack_pair(u[pr]), _stack_pair(ld(v_ref, c, pr))], axis=0))
               for pr, j in enumerate(idx)]
        for pr in range(n_pairs):
            p_last = pl_ref[c * CHUNK:c * CHUNK + 1, pr * pair_w:(pr + 1) * pair_w]
            upd = _dot(jnp.concatenate([u[pr], ld(v_ref, c, pr)], axis=0),
                       jnp.concatenate([ld(bdc_ref, c, pr), ld(kdc_ref, c, pr)], axis=0), TN)
            s_ref[pr] = s[pr] * p_last + jnp.where(same_head, upd, 0.0)
            y_ref[c * CHUNK:(c + 1) * CHUNK, pr * pair_w:(pr + 1) * pair_w] = y_c[pr]

    y = y_ref[...]
    inv_n = 1.0 / B_HEAD_DIM
    mu = gsum(y) * inv_n
    yc = y - mu
    var = gsum(yc * yc) * inv_n
    yn = yc * lax.rsqrt(var + B_LN_EPS) * lnw_ref[...] + lnb_ref[...]
    o_ref[...] = ((yn + bonus_ref[...]) * gate_ref[...]).astype(o_ref.dtype)


def _rwkv(pb, mu, w0, w2, a0, a2, g2, k_k, k_a, r_k, ln_w, ln_b, batch):
    m, in_w = pb.shape
    width = w0.shape[0]
    rows = MIX_ROWS
    tblocks = m // batch // rows
    lane_head = np.arange(width) // B_HEAD_DIM
    bd = jnp.asarray(lane_head[:, None] == lane_head[None, :], BF16)
    lcat = _chunk_sum_matrix(rows)
    row = lambda p: p.reshape(1, -1).astype(F32)
    row_spec = lambda w: pl.BlockSpec((rows, w), lambda b, i: (b * tblocks + i, 0))
    vec = _const_spec((1, width))
    act = lambda: pltpu.VMEM((rows, width), F32)
    pair_w = 2 * B_HEAD_DIM
    n_keys = (rows // CHUNK) * (width // pair_w)
    return pl.pallas_call(
        _rwkv_kernel,
        grid=(batch, tblocks),
        in_specs=[row_spec(in_w), _const_spec((1, in_w)), vec, _const_spec(w2.shape), vec,
                  _const_spec(a2.shape), _const_spec(g2.shape), vec, vec, vec, vec, vec,
                  _const_spec(bd.shape), _const_spec(lcat.shape)],
        out_specs=row_spec(width),
        out_shape=jax.ShapeDtypeStruct((m, width), BF16),
        scratch_shapes=[pltpu.VMEM((rows + SUBLANES, in_w), F32),
                        pltpu.VMEM((width // pair_w, pair_w, pair_w), F32)]
                       + [act() for _ in range(11)]
                       + [pltpu.VMEM((n_keys, 2 * CHUNK, pair_w), BF16), pltpu.VMEM((n_keys, CHUNK, pair_w), F32),
                          pltpu.VMEM((n_keys, CHUNK, 4 * CHUNK), BF16)],
        compiler_params=_params(("arbitrary", "arbitrary")),
        name="rwkv_mixer",
    )(pb, row(mu), row(w0), w2, row(a0), a2, g2, row(k_k), row(k_a), row(r_k), row(ln_w), row(ln_b),
      bd, lcat)


def _rglru_kernel(gx_ref, cw_ref, cb_ref, wa_ref, ba_ref, wx_ref, bx_ref, lam_ref, o_ref, xp_ref, h_ref):
    rows = gx_ref.shape[0]
    width = o_ref.shape[1]
    tb = pl.program_id(1)

    @pl.when(tb == 0)
    def _():
        xp_ref[0:SUBLANES, :] = jnp.zeros((SUBLANES, width), F32)
        h_ref[...] = jnp.zeros(h_ref.shape, F32)

    xp_ref[SUBLANES:SUBLANES + rows, :] = gx_ref[:, width:2 * width]
    xc = cb_ref[...] + cw_ref[C_CONV - 1:C_CONV, :] * _shifted_rows(xp_ref, 0, rows)
    for back in range(1, C_CONV):
        xc = xc + cw_ref[C_CONV - 1 - back:C_CONV - back, :] * _shifted_rows(xp_ref, back, rows)
    xp_ref[0:SUBLANES, :] = xp_ref[rows:rows + SUBLANES, :]

    gate_r = _sigmoid(_dot(xc, wa_ref[...]) + ba_ref[...])
    gate_i = _sigmoid(_dot(xc, wx_ref[...]) + bx_ref[...])
    log_a = -C_GATE_SCALE * gate_r * _softplus(-lam_ref[...])
    a = jnp.exp(log_a)
    d = jnp.sqrt(jnp.tanh(-log_a) * (a * a + 1.0)) * (gate_i * xc)

    t_idx = _iota((rows, width), 0)
    shift = 1
    while shift < SUBLANES:
        keep = t_idx >= shift
        d = jnp.where(keep, a * pltpu.roll(d, shift, 0) + d, d)
        a = jnp.where(keep, a * pltpu.roll(a, shift, 0), a)
        shift *= 2
    while shift < rows:
        d = jnp.concatenate([d[:shift], a[shift:] * d[:rows - shift] + d[shift:]], axis=0)
        a = jnp.concatenate([a[:shift], a[shift:] * a[:rows - shift]], axis=0)
        shift *= 2
    h = a * h_ref[0:1, :] + d
    h_ref[...] = jnp.broadcast_to(h[rows - 1:rows, :], h_ref.shape)
    o_ref[...] = (h * _gelu_tanh(gx_ref[:, 0:width])).astype(o_ref.dtype)


def _rglru(gx, conv_w, conv_b, wa, ba, wx, bx, lam, batch):
    m = gx.shape[0]
    width = gx.shape[1] // 2
    rows = MIX_ROWS
    tblocks = m // batch // rows
    row = lambda p: p.reshape(1, -1).astype(F32)
    blockdiag = lambda w: jax.scipy.linalg.block_diag(*[w[i] for i in range(w.shape[0])]).astype(BF16)
    row_spec = lambda w: pl.BlockSpec((rows, w), lambda b, i: (b * tblocks + i, 0))
    vec = _const_spec((1, width))
    mat = _const_spec((width, width))
    return pl.pallas_call(
        _rglru_kernel,
        grid=(batch, tblocks),
        in_specs=[row_spec(2 * width), _const_spec(conv_w.shape), vec, mat, vec, mat, vec, vec],
        out_specs=row_spec(width),
        out_shape=jax.ShapeDtypeStruct((m, width), BF16),
        scratch_shapes=[pltpu.VMEM((rows + SUBLANES, width), F32), pltpu.VMEM((SUBLANES, width), F32)],
        compiler_params=_params(("arbitrary", "arbitrary")),
        name="rglru_mixer",
    )(gx, conv_w, row(conv_b), blockdiag(wa), row(ba), blockdiag(wx), row(bx), row(lam))


_LOG_GAMMA = tuple(math.log(1.0 - 2.0 ** (-5.0 - h)) for h in range(D_HEADS))


def _per_head(lane_head, values):
    out = jnp.full(lane_head.shape, values[-1], F32)
    for h in range(len(values) - 2, -1, -1):
        out = jnp.where(lane_head == h, values[h], out)
    return out


def _retention_kernel(qk_ref, vg_ref, cos_ref, sin_ref, gw_ref, gb_ref, o_ref, s_ref, dm_ref):
    rows = qk_ref.shape[0]
    kw = qk_ref.shape[1] // 2
    vw = o_ref.shape[1]

    @pl.when((pl.program_id(0) == 0) & (pl.program_id(1) == 0))
    def _():
        rel = (_iota((rows, rows), 0) - _iota((rows, rows), 1)).astype(F32)
        for h in range(D_HEADS):
            dm_ref[h] = jnp.where(rel >= 0, jnp.exp(jnp.maximum(rel, 0.0) * _LOG_GAMMA[h]), 0.0)

    @pl.when(pl.program_id(1) == 0)
    def _():
        s_ref[...] = jnp.zeros(s_ref.shape, F32)

    def rotary(x):
        even = _mod(_iota((rows, LANES), 1), 2) == 0
        parts = []
        for c in range(kw // LANES):
            xc = x[:, c * LANES:(c + 1) * LANES]
            parts.append(jnp.where(even, pltpu.roll(xc, LANES - 1, 1), pltpu.roll(xc, 1, 1)))
        return x * cos_ref[...] + jnp.concatenate(parts, axis=1) * sin_ref[...]

    q = rotary(qk_ref[:, 0:kw])
    k = rotary(qk_ref[:, kw:2 * kw]) * (D_KEY_DIM ** -0.5)
    v = vg_ref[:, 0:vw]
    k_head = _div(_iota((1, kw), 1), D_KEY_DIM)
    lg_k = _per_head(k_head, _LOG_GAMMA)
    lg_v = _per_head(_div(_iota((1, vw), 1), D_VAL_DIM), _LOG_GAMMA)
    t_idx = _iota((rows, 1), 0).astype(F32)

    o_cross = _dot(q * jnp.exp((t_idx + 1.0) * lg_k), s_ref[...])
    inner = []
    for h in range(D_HEADS):
        scores = _dot(jnp.where(k_head == h, q, 0.0), k, NT) * dm_ref[h]
        inner.append(_dot(scores, v[:, h * D_VAL_DIM:(h + 1) * D_VAL_DIM]))
    o = o_cross + jnp.concatenate(inner, axis=1)

    k_dec = k * jnp.exp((rows - 1.0 - t_idx) * lg_k)
    same_head = _div(_iota((kw, vw), 0), D_KEY_DIM) == _div(_iota((kw, vw), 1), D_VAL_DIM)
    s_ref[...] = s_ref[...] * jnp.exp(float(rows) * lg_v) + jnp.where(same_head, _dot(k_dec, v, TN), 0.0)

    inv_n = 1.0 / D_VAL_DIM
    mu = _head_sums(o, D_VAL_DIM) * inv_n
    oc = o - mu
    var = _head_sums(oc * oc, D_VAL_DIM) * inv_n
    on = oc * lax.rsqrt(var + EPS) * gw_ref[...] + gb_ref[...]
    o_ref[...] = (_silu(vg_ref[:, vw:2 * vw]) * on).astype(o_ref.dtype)


def _retention(qk, vg, gn_w, gn_b, batch):
    m = qk.shape[0]
    kw = qk.shape[1] // 2
    vw = vg.shape[1] // 2
    rows = MIX_ROWS
    seq = m // batch
    tblocks = seq // rows
    inv = 1.0 / (ROPE_BASE ** jnp.linspace(0.0, 1.0, D_KEY_DIM // 2, dtype=F32))
    ang = jnp.arange(seq).astype(F32)[:, None] * inv[None, :]
    cos = jnp.tile(jnp.repeat(jnp.cos(ang), 2, axis=1), (1, kw // D_KEY_DIM))
    sin = jnp.tile(jnp.stack([-jnp.sin(ang), jnp.sin(ang)], axis=-1).reshape(seq, D_KEY_DIM),
                   (1, kw // D_KEY_DIM))
    row = lambda p: p.reshape(1, -1).astype(F32)
    row_spec = lambda w: pl.BlockSpec((rows, w), lambda b, i: (b * tblocks + i, 0))
    tab_spec = pl.BlockSpec((rows, kw), lambda b, i: (i, 0))
    return pl.pallas_call(
        _retention_kernel,
        grid=(batch, tblocks),
        in_specs=[row_spec(2 * kw), row_spec(2 * vw), tab_spec, tab_spec, _const_spec((1, vw)),
                  _const_spec((1, vw))],
        out_specs=row_spec(vw),
        out_shape=jax.ShapeDtypeStruct((m, vw), BF16),
        scratch_shapes=[pltpu.VMEM((kw, vw), F32), pltpu.VMEM((D_HEADS, rows, rows), F32)],
        compiler_params=_params(("arbitrary", "arbitrary")),
        name="retention_mixer",
    )(qk, vg, cos, sin, row(gn_w), row(gn_b))


def _ffn_kernel(x_ref, xh_ref, oa_ref, oah_ref, ob_ref, obh_ref, wo_ref, g_ref, wu_ref, wv_ref, wc_ref, wd_ref,
                *rest, seq_blocks, final):
    if final:
        gf_ref, y_ref, hc_ref, u_ref = rest
    else:
        y_ref, hc_ref, u_ref = rest
    rows = x_ref.shape[0]
    half = oa_ref.shape[1]
    c = pl.program_id(1)

    @pl.when(c == 0)
    def _():
        def mixed(x, oa, ob):
            return (x + jnp.dot(oa, wo_ref[0:half, :], preferred_element_type=F32)
                    + jnp.dot(ob, wo_ref[half:2 * half, :], preferred_element_type=F32))

        x1 = mixed(x_ref[...], oa_ref[...], ob_ref[...])
        x1h = mixed(xh_ref[...], oah_ref[...], obh_ref[...])
        starts_seq = (pl.program_id(0) % seq_blocks) == 0
        y_ref[...] = x1
        hc_ref[0:BF16_ROWS, :] = (_rms(x1h, g_ref[...]) * jnp.where(starts_seq, 0.0, 1.0)).astype(BF16)
        hc_ref[BF16_ROWS:BF16_ROWS + rows, :] = _rms(x1, g_ref[...]).astype(BF16)

    u_ref[...] = jnp.dot(hc_ref[...], wu_ref[...], preferred_element_type=F32)
    v = jnp.dot(hc_ref[BF16_ROWS:BF16_ROWS + rows, :], wv_ref[...], preferred_element_type=F32)
    uc = wc_ref[FFN_CONV - 1:FFN_CONV, :] * u_ref[pl.ds(BF16_ROWS, rows), :]
    for back in range(1, FFN_CONV):
        uc = uc + wc_ref[FFN_CONV - 1 - back:FFN_CONV - back, :] * u_ref[pl.ds(BF16_ROWS - back, rows), :]
    gl = (_silu(uc) * v).astype(BF16)
    y_ref[...] += jnp.dot(gl, wd_ref[...], preferred_element_type=F32)

    if final:
        @pl.when(c == pl.num_programs(1) - 1)
        def _():
            y_ref[...] = _rms(y_ref[...], gf_ref[...])


def _ffn(x, oa, ob, w_out, norm_w, w_up, w_conv, w_down, seq, final_norm=None):
    m, d = x.shape
    half = oa.shape[1]
    ffn = w_down.shape[0]
    tm = FFN_ROWS
    fc = ffn // FFN_SPLIT
    halo = BF16_ROWS
    final = final_norm is not None
    row_spec = lambda w: pl.BlockSpec((tm, w), lambda i, c: (i, 0))
    halo_spec = lambda w: pl.BlockSpec((halo, w), lambda i, c: (jnp.maximum(i * (tm // halo) - 1, 0), 0))
    row = lambda p: p.reshape(1, -1).astype(F32)
    weight_mode = pl.Buffered(1) if FFN_SPLIT == 1 else None
    in_specs = [row_spec(d), halo_spec(d), row_spec(half), halo_spec(half), row_spec(half), halo_spec(half),
                _const_spec(w_out.shape), _const_spec((1, d)),
                pl.BlockSpec((d, fc), lambda i, c: (0, c), pipeline_mode=weight_mode),
                pl.BlockSpec((d, fc), lambda i, c: (0, FFN_SPLIT + c), pipeline_mode=weight_mode),
                pl.BlockSpec((FFN_CONV, fc), lambda i, c: (0, c)),
                pl.BlockSpec((fc, d), lambda i, c: (c, 0), pipeline_mode=weight_mode)]
    args = [x, x, oa, oa, ob, ob, w_out, row(norm_w), w_up, w_up, w_conv, w_down]
    if final:
        in_specs.append(_const_spec((1, d)))
        args.append(row(final_norm))
    return pl.pallas_call(
        functools.partial(_ffn_kernel, seq_blocks=seq // tm, final=final),
        grid=(m // tm, FFN_SPLIT),
        in_specs=in_specs,
        out_specs=row_spec(d),
        out_shape=jax.ShapeDtypeStruct((m, d), F32),
        scratch_shapes=[pltpu.VMEM((tm + halo, d), BF16), pltpu.VMEM((tm + halo, fc), F32)],
        compiler_params=_params(("parallel", "arbitrary")),
        name="outproj_ffn_final" if final else "outproj_ffn",
    )(*args)


def kernel(x, l0_norm1, l0_w_in, l0_a_conv, l0_a_A_log, l0_a_dt_bias, l0_a_norm, l0_b_mu, l0_b_w0, l0_b_w2, l0_b_a0, l0_b_a2, l0_b_g2, l0_b_k_k, l0_b_k_a, l0_b_r_k, l0_b_ln_w, l0_b_ln_b, l0_w_out, l0_norm2, l0_ffn_up, l0_ffn_conv, l0_ffn_down, l1_norm1, l1_w_in, l1_c_conv_w, l1_c_conv_b, l1_c_wa, l1_c_ba, l1_c_wx, l1_c_bx, l1_c_lambda, l1_d_gn_w, l1_d_gn_b, l1_w_out, l1_norm2, l1_ffn_up, l1_ffn_conv, l1_ffn_down, final_norm):
    batch, seq, d = x.shape
    xf = x.reshape(batch * seq, d)
    bf = lambda w: w.astype(BF16)

    a_w = l0_a_conv.shape[1] // 3
    a_heads = l0_a_A_log.shape[0]
    b_w = l0_b_w0.shape[0]
    lora_w, lora_a = l0_b_w2.shape[0], l0_b_a2.shape[0]
    a_in = 4 * a_w + 2 * a_heads
    o_b = a_in + 3 * b_w
    zcols = lambda n: jnp.zeros((d, n), BF16)
    w_in0 = jnp.concatenate([
        bf(l0_w_in[:, 0:4 * a_w]),
        bf(l0_w_in[:, 4 * a_w:a_in]), zcols(LANES - 2 * a_heads),
        bf(l0_w_in[:, a_in:o_b]),
        bf(l0_w_in[:, o_b:o_b + lora_w]), zcols(B_LORA_PAD - lora_w),
        bf(l0_w_in[:, o_b + lora_w:o_b + lora_w + lora_a]), zcols(B_LORA_PAD - lora_a),
        bf(l0_w_in[:, o_b + lora_w + lora_a:])], axis=1)
    zrow = lambda n: jnp.zeros((n,), F32)
    mu0 = jnp.concatenate([
        l0_b_mu[0:3 * b_w + lora_w], zrow(B_LORA_PAD - lora_w),
        l0_b_mu[3 * b_w + lora_w:3 * b_w + lora_w + lora_a], zrow(B_LORA_PAD - lora_a),
        l0_b_mu[3 * b_w + lora_w + lora_a:]])
    pad_rows = lambda w: jnp.concatenate([w, jnp.zeros((B_LORA_PAD - w.shape[0], w.shape[1]), F32)], axis=0)
    b_in = mu0.shape[0]

    qkvz, ba, pb = _norm_proj(xf, l0_norm1, w_in0, (4 * a_w, LANES, b_in), "norm_proj0")
    o_a = _gdn(qkvz, ba, l0_a_conv, l0_a_A_log, l0_a_dt_bias, l0_a_norm, batch)
    o_bb = _rwkv(pb, mu0, l0_b_w0, bf(pad_rows(l0_b_w2)), l0_b_a0, bf(pad_rows(l0_b_a2)), bf(l0_b_g2),
                 l0_b_k_k, l0_b_k_a, l0_b_r_k, l0_b_ln_w, l0_b_ln_b, batch)
    x1 = _ffn(xf, o_a, o_bb, bf(l0_w_out), l0_norm2, bf(l0_ffn_up), l0_ffn_conv, bf(l0_ffn_down), seq)

    c_w = l1_c_lambda.shape[0]
    d_w = l1_d_gn_w.shape[0]
    qk_w = l1_w_in.shape[1] - 2 * c_w - 2 * d_w
    gx, qk, vg = _norm_proj(x1, l1_norm1, bf(l1_w_in), (2 * c_w, qk_w, 2 * d_w), "norm_proj1")
    o_c = _rglru(gx, l1_c_conv_w, l1_c_conv_b, l1_c_wa, l1_c_ba, l1_c_wx, l1_c_bx, l1_c_lambda, batch)
    o_d = _retention(qk, vg, l1_d_gn_w, l1_d_gn_b, batch)
    y = _ffn(x1, o_c, o_d, bf(l1_w_out), l1_norm2, bf(l1_ffn_up), l1_ffn_conv, bf(l1_ffn_down), seq,
             final_norm=final_norm)
    return y.reshape(batch, seq, d)
```

```python
import functools
import math

import numpy as np
import jax
import jax.numpy as jnp
from jax import lax
from jax.experimental import pallas as pl
from jax.experimental.pallas import tpu as pltpu

F32 = jnp.float32
BF16 = jnp.bfloat16

EPS = 1e-6
CHUNK = 64
A_HEAD_DIM = 128
A_CONV = 4
B_HEAD_DIM = 64
B_LN_EPS = 64e-5
C_GATE_SCALE = 8.0
C_CONV = 4
D_KEY_DIM = 64
D_VAL_DIM = 128
D_HEADS = 4
ROPE_BASE = 10000.0
FFN_CONV = 3

LANES = 128
SUBLANES = 8
BF16_ROWS = 16
VMEM_LIMIT = 56 * 1024 * 1024

MIX_ROWS = 256
PROJ_ROWS = 256
FFN_ROWS = 512

NN = (((1,), (0,)), ((), ()))
NT = (((1,), (1,)), ((), ()))
TN = (((0,), (0,)), ((), ()))


def _pieces(x, n):
    if x.dtype == BF16:
        return [x]
    out, r = [], x
    for i in range(n):
        p = r.astype(BF16)
        out.append(p)
        if i + 1 < n:
            r = r - p.astype(F32)
    return out


def _dot(a, b, dims=NN, pa=1, pb=1):
    ap, bp = _pieces(a, pa), _pieces(b, pb)
    order = max(len(ap), len(bp))
    acc = None
    for i in reversed(range(len(ap))):
        for j in reversed(range(len(bp))):
            if i + j < order:
                t = lax.dot_general(ap[i], bp[j], dims, preferred_element_type=F32)
                acc = t if acc is None else acc + t
    return acc


def _sigmoid(x):
    return 1.0 / (1.0 + jnp.exp(-x))


def _silu(x):
    return x * _sigmoid(x)


def _softplus(x):
    return jnp.maximum(x, 0.0) + jnp.log1p(jnp.exp(-jnp.abs(x)))


def _gelu_tanh(x):
    return 0.5 * x * (1.0 + jnp.tanh(math.sqrt(2.0 / math.pi) * (x + 0.044715 * (x * x * x))))


def _rms(x, g):
    return x * lax.rsqrt(jnp.mean(x * x, axis=-1, keepdims=True) + EPS) * g


def _iota(shape, dim):
    return lax.broadcasted_iota(jnp.int32, shape, dim)


def _div(i, n):
    assert n & (n - 1) == 0
    return i >> (n.bit_length() - 1)


def _mod(i, n):
    assert n & (n - 1) == 0
    return i & (n - 1)


def _pair_masks():
    t, s = _iota((CHUNK, 2 * CHUNK), 0), _mod(_iota((CHUNK, 2 * CHUNK), 1), CHUNK)
    return s < t, s <= t


def _stack_heads(x, n):
    head = _div(_iota(x.shape, 1), x.shape[1] // n)
    return jnp.concatenate([jnp.where(head == h, x, 0.0) for h in range(n)], axis=0)


def _stack_pair(x):
    return _stack_heads(x, 2)


def _inv_unit_lower(n_mats):
    shape = n_mats[0].shape
    heads = shape[1] // CHUNK
    stack = functools.partial(_stack_heads, n=heads)
    eye = (_mod(_iota(shape, 1), CHUNK) == _iota(shape, 0)).astype(F32)
    ts = [eye + n for n in n_mats]
    ps = [_dot(n, stack(n)) for n in n_mats]
    levels = int(math.log2(CHUNK)) - 1
    for level in range(levels):
        last = level == levels - 1
        lhs = ts if last else [jnp.concatenate([t, p], axis=0) for t, p in zip(ts, ps)]
        prod = [_dot(l, stack(p)) for l, p in zip(lhs, ps)]
        ts = [t + pr[0:CHUNK] for t, pr in zip(ts, prod)]
        if not last:
            ps = [pr[CHUNK:2 * CHUNK] for pr in prod]
    return ts


def _head_sums(x, dim):
    assert dim in (LANES, LANES // 2)
    low = _iota((x.shape[0], LANES), 1) < dim
    parts = []
    for c in range(x.shape[1] // LANES):
        xc = x[:, c * LANES:(c + 1) * LANES]
        total = jnp.sum(xc, axis=-1, keepdims=True)
        if dim == LANES:
            parts.append(jnp.broadcast_to(total, xc.shape))
        else:
            first = jnp.sum(jnp.where(low, xc, 0.0), axis=-1, keepdims=True)
            parts.append(jnp.where(low, first, total - first))
    return jnp.concatenate(parts, axis=1)


def _head_expand(cols, first, heads, dim):
    return jnp.concatenate([jnp.broadcast_to(cols[:, first + h:first + h + 1], (cols.shape[0], dim))
                            for h in range(heads)], axis=1)


def _shifted_rows(xp_ref, back, rows):
    if back == 0:
        return xp_ref[SUBLANES:SUBLANES + rows, :]
    return pltpu.roll(xp_ref[...], back, 0)[SUBLANES:SUBLANES + rows, :]


def _params(sem):
    return pltpu.CompilerParams(dimension_semantics=sem, vmem_limit_bytes=VMEM_LIMIT)


def _const_spec(shape):
    return pl.BlockSpec(shape, lambda *_: (0,) * len(shape))


def _chunk_sum_matrix(rows):
    t = np.arange(rows)
    same = (t[:, None] // CHUNK) == (t[None, :] // CHUNK)
    return jnp.asarray(np.concatenate([same & (t[None, :] <= t[:, None]), same], axis=0), BF16)


def _norm_proj_kernel(x_ref, g_ref, *refs, splits):
    w_refs, o_refs = refs[:len(splits)], refs[len(splits):]
    h = _rms(x_ref[...], g_ref[...]).astype(BF16)
    outs = iter(o_refs)
    for w_ref, cols in zip(w_refs, splits):
        off = 0
        for n in cols:
            next(outs)[...] = jnp.dot(h, w_ref[:, off:off + n], preferred_element_type=F32)
            off += n


def _norm_proj(x, g, ws, splits, name):
    m, d = x.shape
    tm = PROJ_ROWS
    widths = [n for cols in splits for n in cols]
    return pl.pallas_call(
        functools.partial(_norm_proj_kernel, splits=splits),
        grid=(m // tm,),
        in_specs=[pl.BlockSpec((tm, d), lambda i: (i, 0)), _const_spec((1, d))] + [_const_spec(w.shape) for w in ws],
        out_specs=[pl.BlockSpec((tm, n), lambda i: (i, 0)) for n in widths],
        out_shape=[jax.ShapeDtypeStruct((m, n), F32) for n in widths],
        compiler_params=_params(("parallel",)),
        name=name,
    )(x, g.reshape(1, d), *ws)


def _gdn_kernel(qkvz_ref, ba_ref, cw_ref, alog_ref, dtb_ref, nw_ref, lcat_ref,
                o_ref, xp_ref, s_ref, qs_ref, k_ref, kb_ref, vb_ref, kbg_ref, qd_ref, kd_ref, gc_ref,
                gl_ref, oc_ref, us_ref, wq_ref, qkm_ref):
    rows = qkvz_ref.shape[0]
    width = o_ref.shape[1]
    tb = pl.program_id(1)

    @pl.when(tb == 0)
    def _():
        xp_ref[0:SUBLANES, :] = jnp.zeros((SUBLANES, xp_ref.shape[1]), F32)
        s_ref[...] = jnp.zeros(s_ref.shape, F32)

    xp_ref[SUBLANES:SUBLANES + rows, :] = qkvz_ref[:, 0:3 * width]
    acc = cw_ref[A_CONV - 1:A_CONV, :] * _shifted_rows(xp_ref, 0, rows)
    for back in range(1, A_CONV):
        acc = acc + cw_ref[A_CONV - 1 - back:A_CONV - back, :] * _shifted_rows(xp_ref, back, rows)
    xp_ref[0:SUBLANES, :] = xp_ref[rows:rows + SUBLANES, :]
    act = _silu(acc)
    q, k, v = act[:, 0:width], act[:, width:2 * width], act[:, 2 * width:3 * width]

    def l2n(x):
        return x * lax.rsqrt(_head_sums(x * x, A_HEAD_DIM) + EPS)

    ba = ba_ref[...]
    n_heads = width // A_HEAD_DIM
    bg = jnp.where(_iota(ba.shape, 1) < n_heads, _sigmoid(ba),
                   -jnp.exp(alog_ref[...]) * _softplus(ba + dtb_ref[...]))
    ct = _dot(lcat_ref[...], bg, pb=3)
    beta_f = _head_expand(bg, 0, n_heads, A_HEAD_DIM)
    gc_f = _head_expand(ct[0:rows], n_heads, n_heads, A_HEAD_DIM)
    gl_f = _head_expand(ct[rows:2 * rows], n_heads, n_heads, A_HEAD_DIM)

    kn = l2n(k)
    qs = l2n(q) * (A_HEAD_DIM ** -0.5)
    kb = kn * beta_f
    e_gc = jnp.exp(gc_f)
    qs_ref[...] = qs
    k_ref[...] = kn
    kb_ref[...] = kb
    vb_ref[...] = v * beta_f
    kbg_ref[...] = kb * e_gc
    qd_ref[...] = qs * e_gc
    kd_ref[...] = kn * jnp.exp(gl_f - gc_f)
    gc_ref[...] = gc_f
    gl_ref[...] = gl_f

    pair_w = 2 * A_HEAD_DIM
    assert A_HEAD_DIM == 2 * CHUNK
    n_pairs = width // pair_w
    n_chunks = rows // CHUNK
    strict, lower = _pair_masks()
    first_head = _iota((CHUNK, 2 * CHUNK), 1) < CHUNK
    same_head = (_iota((pair_w, pair_w), 0) < A_HEAD_DIM) == (_iota((pair_w, pair_w), 1) < A_HEAD_DIM)

    def ld(ref, c, p):
        return ref[c * CHUNK:(c + 1) * CHUNK, p * pair_w:(p + 1) * pair_w]

    keys = [(c, p) for c in range(n_chunks) for p in range(n_pairs)]
    g = [_dot(jnp.concatenate([ld(kb_ref, *k), ld(qs_ref, *k)], axis=0), _stack_pair(ld(k_ref, *k)), NT)
         for k in keys]
    n_ms = []
    for i, k in enumerate(keys):
        gcp = ld(gc_ref, *k)
        g_t = jnp.where(first_head, gcp[:, 0:A_HEAD_DIM], gcp[:, A_HEAD_DIM:pair_w])
        g_s = jnp.concatenate([gcp[:, 0:A_HEAD_DIM], gcp[:, A_HEAD_DIM:pair_w]], axis=0).T[0:CHUNK]
        dec = jnp.exp(jnp.where(lower, g_t - g_s, 0.0))
        n_ms.append(jnp.where(strict, -(g[i][0:CHUNK] * dec), 0.0))
        qkm_ref[i] = jnp.where(lower, g[i][CHUNK:2 * CHUNK] * dec, 0.0).astype(BF16)
    t_ms = _inv_unit_lower(n_ms)
    for i, k in enumerate(keys):
        uw = _dot(t_ms[i], jnp.concatenate([_stack_pair(ld(vb_ref, *k)), _stack_pair(ld(kbg_ref, *k))], axis=1))
        us_ref[i] = uw[:, 0:pair_w]
        wq_ref[i] = jnp.concatenate([uw[:, pair_w:2 * pair_w], ld(qd_ref, *k)], axis=0).astype(BF16)

    for c in range(n_chunks):
        idx = [c * n_pairs + p for p in range(n_pairs)]
        s = [s_ref[p] for p in range(n_pairs)]
        x = [_dot(wq_ref[i], s[p]) for p, i in enumerate(idx)]
        v_new = [us_ref[i] - x[p][0:CHUNK] for p, i in enumerate(idx)]
        o_c = [x[p][CHUNK:2 * CHUNK] + _dot(qkm_ref[i], _stack_pair(v_new[p])) for p, i in enumerate(idx)]
        for p in range(n_pairs):
            g_last = jnp.exp(gl_ref[c * CHUNK:c * CHUNK + 1, p * pair_w:(p + 1) * pair_w])
            s_ref[p] = s[p] * g_last + jnp.where(same_head, _dot(ld(kd_ref, c, p), v_new[p], TN), 0.0)
            oc_ref[c * CHUNK:(c + 1) * CHUNK, p * pair_w:(p + 1) * pair_w] = o_c[p]

    o = oc_ref[...]
    ms = _head_sums(o * o, A_HEAD_DIM) * (1.0 / A_HEAD_DIM)
    z = qkvz_ref[:, 3 * width:4 * width]
    o_ref[...] = (o * lax.rsqrt(ms + EPS) * nw_ref[...] * _silu(z)).astype(o_ref.dtype)


def _gdn(qkvz, ba, conv_w, a_log, dt_bias, norm_w, batch):
    m = qkvz.shape[0]
    width = qkvz.shape[1] // 4
    n_heads = width // A_HEAD_DIM
    rows = MIX_ROWS
    tblocks = m // batch // rows
    pad = lambda p: jnp.zeros((1, LANES), F32).at[0, n_heads:2 * n_heads].set(p)
    lcat = _chunk_sum_matrix(rows)
    row_spec = lambda w: pl.BlockSpec((rows, w), lambda b, i: (b * tblocks + i, 0))
    act = lambda: pltpu.VMEM((rows, width), F32)
    pair_w = 2 * A_HEAD_DIM
    n_keys = (rows // CHUNK) * (width // pair_w)
    return pl.pallas_call(
        _gdn_kernel,
        grid=(batch, tblocks),
        in_specs=[row_spec(4 * width), row_spec(LANES), _const_spec(conv_w.shape), _const_spec((1, LANES)),
                  _const_spec((1, LANES)), _const_spec((1, width)), _const_spec(lcat.shape)],
        out_specs=row_spec(width),
        out_shape=jax.ShapeDtypeStruct((m, width), BF16),
        scratch_shapes=[pltpu.VMEM((rows + SUBLANES, 3 * width), F32),
                        pltpu.VMEM((width // pair_w, pair_w, pair_w), F32)]
                       + [act() for _ in range(10)]
                       + [pltpu.VMEM((n_keys, CHUNK, pair_w), F32), pltpu.VMEM((n_keys, 2 * CHUNK, pair_w), BF16),
                          pltpu.VMEM((n_keys, CHUNK, 2 * CHUNK), BF16)],
        compiler_params=_params(("arbitrary", "arbitrary")),
        name="gdn_mixer",
    )(qkvz, ba, conv_w, pad(a_log), pad(dt_bias), jnp.tile(norm_w, n_heads).reshape(1, width), lcat)


def _rwkv_kernel(pb_ref, mu_ref, w0_ref, w2_ref, a0_ref, a2_ref, g2_ref, kk_ref, ka_ref, rk_ref, lnw_ref,
                 lnb_ref, lcat_ref, o_ref, xp_ref, s_ref, at_ref, rt_ref, bt_ref, kt_ref,
                 v_ref, bdc_ref, kdc_ref, pl_ref, y_ref, bonus_ref, gate_ref, lx_ref, tw_ref, mc_ref):
    rows = pb_ref.shape[0]
    width = o_ref.shape[1]
    tb = pl.program_id(1)

    @pl.when(tb == 0)
    def _():
        xp_ref[0:SUBLANES, :] = jnp.zeros((SUBLANES, xp_ref.shape[1]), F32)
        s_ref[...] = jnp.zeros(s_ref.shape, F32)

    p = pb_ref[...]
    xp_ref[SUBLANES:SUBLANES + rows, :] = p
    prev = _shifted_rows(xp_ref, 1, rows)
    xp_ref[0:SUBLANES, :] = xp_ref[rows:rows + SUBLANES, :]
    m = p + (prev - p) * mu_ref[...]
    r, kr, vr = m[:, 0:width], m[:, width:2 * width], m[:, 2 * width:3 * width]
    o1 = 3 * width
    o2 = o1 + w2_ref.shape[0]
    o3 = o2 + a2_ref.shape[0]
    w_lo, a_lo, g_lo = m[:, o1:o2], m[:, o2:o3], m[:, o3:]

    w_log = -_softplus(-(w0_ref[...] + _dot(jnp.tanh(w_lo), w2_ref[...]))) - 0.5
    lw = -jnp.exp(w_log)
    a_lr = _sigmoid(a0_ref[...] + _dot(a_lo, a2_ref[...]))
    gate_ref[...] = _dot(_sigmoid(g_lo), g2_ref[...])

    def gsum(x):
        return _head_sums(x, B_HEAD_DIM)

    kk = kr * kk_ref[...]
    k_mod = kr * (1.0 + (a_lr - 1.0) * ka_ref[...])
    kk = kk * lax.rsqrt(gsum(kk * kk) + EPS)
    a_vec = -kk
    b_vec = kk * a_lr
    bonus_ref[...] = gsum(r * k_mod * rk_ref[...]) * vr

    ct = _dot(lcat_ref[...], lw, pb=2)
    cum, tot = ct[0:rows], ct[rows:2 * rows]
    e_neg = jnp.exp(-cum)
    e_dec = jnp.exp(tot - cum)
    rt_ref[...] = r * jnp.exp(cum)
    at_ref[...] = a_vec * jnp.exp(cum - lw)
    bt_ref[...] = b_vec * e_neg
    kt_ref[...] = k_mod * e_neg
    bdc_ref[...] = b_vec * e_dec
    kdc_ref[...] = k_mod * e_dec
    pl_ref[...] = jnp.exp(tot)
    v_ref[...] = vr

    pair_w = 2 * B_HEAD_DIM
    n_pairs = width // pair_w
    n_chunks = rows // CHUNK
    strict, lower = _pair_masks()
    lower2 = jnp.concatenate([lower, lower], axis=1)
    same_head = (_iota((pair_w, pair_w), 0) < B_HEAD_DIM) == (_iota((pair_w, pair_w), 1) < B_HEAD_DIM)

    def ld(ref, c, pr):
        return ref[c * CHUNK:(c + 1) * CHUNK, pr * pair_w:(pr + 1) * pair_w]

    keys = [(c, pr) for c in range(n_chunks) for pr in range(n_pairs)]
    g = [_dot(jnp.concatenate([ld(at_ref, *k), ld(rt_ref, *k)], axis=0),
              jnp.concatenate([_stack_pair(ld(bt_ref, *k)), _stack_pair(ld(kt_ref, *k))], axis=0), NT)
         for k in keys]
    t_ms = _inv_unit_lower([jnp.where(strict, gi[0:CHUNK, 0:2 * CHUNK], 0.0) for gi in g])
    w1 = [_dot(jnp.where(strict, g[i][0:CHUNK, 2 * CHUNK:4 * CHUNK], 0.0), _stack_pair(ld(v_ref, *k)))
          for i, k in enumerate(keys)]
    taw = [_dot(t_ms[i], jnp.concatenate([_stack_pair(ld(at_ref, *k)), _stack_pair(w1[i])], axis=1))
           for i, k in enumerate(keys)]
    for i, k in enumerate(keys):
        lx_ref[i] = jnp.concatenate([taw[i][:, 0:pair_w], ld(rt_ref, *k)], axis=0).astype(BF16)
        tw_ref[i] = taw[i][:, pair_w:2 * pair_w]
        mc_ref[i] = jnp.where(lower2, g[i][CHUNK:2 * CHUNK], 0.0).astype(BF16)

    for c in range(n_chunks):
        idx = [c * n_pairs + pr for pr in range(n_pairs)]
        s = [s_ref[pr] for pr in range(n_pairs)]
        x = [_dot(lx_ref[j], s[pr], NT) for pr, j in enumerate(idx)]
        u = [x[pr][0:CHUNK] + tw_ref[j] for pr, j in enumerate(idx)]
        y_c = [x[pr][CHUNK:2 * CHUNK]
               + _dot(mc_ref[j], jnp.concatenate([_stack_pair(u[pr]), _stack_pair(ld(v_ref, c, pr))], axis=0))
               for pr, j in enumerate(idx)]
        for pr in range(n_pairs):
            p_last = pl_ref[c * CHUNK:c * CHUNK + 1, pr * pair_w:(pr + 1) * pair_w]
            upd = _dot(jnp.concatenate([u[pr], ld(v_ref, c, pr)], axis=0),
                       jnp.concatenate([ld(bdc_ref, c, pr), ld(kdc_ref, c, pr)], axis=0), TN)
            s_ref[pr] = s[pr] * p_last + jnp.where(same_head, upd, 0.0)
            y_ref[c * CHUNK:(c + 1) * CHUNK, pr * pair_w:(pr + 1) * pair_w] = y_c[pr]

    y = y_ref[...]
    inv_n = 1.0 / B_HEAD_DIM
    mu = gsum(y) * inv_n
    yc = y - mu
    var = gsum(yc * yc) * inv_n
    yn = yc * lax.rsqrt(var + B_LN_EPS) * lnw_ref[...] + lnb_ref[...]
    o_ref[...] = ((yn + bonus_ref[...]) * gate_ref[...]).astype(o_ref.dtype)


def _rwkv(pb, mu, w0, w2, a0, a2, g2, k_k, k_a, r_k, ln_w, ln_b, batch):
    m, in_w = pb.shape
    width = w0.shape[0]
    rows = MIX_ROWS
    tblocks = m // batch // rows
    lcat = _chunk_sum_matrix(rows)
    row = lambda p: p.reshape(1, -1).astype(F32)
    row_spec = lambda w: pl.BlockSpec((rows, w), lambda b, i: (b * tblocks + i, 0))
    vec = _const_spec((1, width))
    act = lambda: pltpu.VMEM((rows, width), F32)
    pair_w = 2 * B_HEAD_DIM
    n_keys = (rows // CHUNK) * (width // pair_w)
    return pl.pallas_call(
        _rwkv_kernel,
        grid=(batch, tblocks),
        in_specs=[row_spec(in_w), _const_spec((1, in_w)), vec, _const_spec(w2.shape), vec,
                  _const_spec(a2.shape), _const_spec(g2.shape), vec, vec, vec, vec, vec,
                  _const_spec(lcat.shape)],
        out_specs=row_spec(width),
        out_shape=jax.ShapeDtypeStruct((m, width), BF16),
        scratch_shapes=[pltpu.VMEM((rows + SUBLANES, in_w), F32),
                        pltpu.VMEM((width // pair_w, pair_w, pair_w), F32)]
                       + [act() for _ in range(11)]
                       + [pltpu.VMEM((n_keys, 2 * CHUNK, pair_w), BF16), pltpu.VMEM((n_keys, CHUNK, pair_w), F32),
                          pltpu.VMEM((n_keys, CHUNK, 4 * CHUNK), BF16)],
        compiler_params=_params(("arbitrary", "arbitrary")),
        name="rwkv_mixer",
    )(pb, row(mu), row(w0), w2, row(a0), a2, g2, row(k_k), row(k_a), row(r_k), row(ln_w), row(ln_b),
      lcat)


def _rglru_kernel(gx_ref, cw_ref, cb_ref, wa_ref, ba_ref, wx_ref, bx_ref, lam_ref, o_ref, xp_ref, h_ref):
    rows = gx_ref.shape[0]
    width = o_ref.shape[1]
    tb = pl.program_id(1)

    @pl.when(tb == 0)
    def _():
        xp_ref[0:SUBLANES, :] = jnp.zeros((SUBLANES, width), F32)
        h_ref[...] = jnp.zeros(h_ref.shape, F32)

    xp_ref[SUBLANES:SUBLANES + rows, :] = gx_ref[:, width:2 * width]
    xc = cb_ref[...] + cw_ref[C_CONV - 1:C_CONV, :] * _shifted_rows(xp_ref, 0, rows)
    for back in range(1, C_CONV):
        xc = xc + cw_ref[C_CONV - 1 - back:C_CONV - back, :] * _shifted_rows(xp_ref, back, rows)
    xp_ref[0:SUBLANES, :] = xp_ref[rows:rows + SUBLANES, :]

    gate_r = _sigmoid(_dot(xc, wa_ref[...]) + ba_ref[...])
    gate_i = _sigmoid(_dot(xc, wx_ref[...]) + bx_ref[...])
    log_a = -C_GATE_SCALE * gate_r * _softplus(-lam_ref[...])
    a = jnp.exp(log_a)
    d = jnp.sqrt(jnp.tanh(-log_a) * (a * a + 1.0)) * (gate_i * xc)

    t_idx = _iota((rows, width), 0)
    shift = 1
    while shift < SUBLANES:
        keep = t_idx >= shift
        d = jnp.where(keep, a * pltpu.roll(d, shift, 0) + d, d)
        a = jnp.where(keep, a * pltpu.roll(a, shift, 0), a)
        shift *= 2
    while shift < rows:
        d = jnp.concatenate([d[:shift], a[shift:] * d[:rows - shift] + d[shift:]], axis=0)
        a = jnp.concatenate([a[:shift], a[shift:] * a[:rows - shift]], axis=0)
        shift *= 2
    h = a * h_ref[0:1, :] + d
    h_ref[...] = jnp.broadcast_to(h[rows - 1:rows, :], h_ref.shape)
    o_ref[...] = (h * _gelu_tanh(gx_ref[:, 0:width])).astype(o_ref.dtype)


def _rglru(gx, conv_w, conv_b, wa, ba, wx, bx, lam, batch):
    m = gx.shape[0]
    width = gx.shape[1] // 2
    rows = MIX_ROWS
    tblocks = m // batch // rows
    row = lambda p: p.reshape(1, -1).astype(F32)
    blockdiag = lambda w: jax.scipy.linalg.block_diag(*[w[i] for i in range(w.shape[0])]).astype(BF16)
    row_spec = lambda w: pl.BlockSpec((rows, w), lambda b, i: (b * tblocks + i, 0))
    vec = _const_spec((1, width))
    mat = _const_spec((width, width))
    return pl.pallas_call(
        _rglru_kernel,
        grid=(batch, tblocks),
        in_specs=[row_spec(2 * width), _const_spec(conv_w.shape), vec, mat, vec, mat, vec, vec],
        out_specs=row_spec(width),
        out_shape=jax.ShapeDtypeStruct((m, width), BF16),
        scratch_shapes=[pltpu.VMEM((rows + SUBLANES, width), F32), pltpu.VMEM((SUBLANES, width), F32)],
        compiler_params=_params(("arbitrary", "arbitrary")),
        name="rglru_mixer",
    )(gx, conv_w, row(conv_b), blockdiag(wa), row(ba), blockdiag(wx), row(bx), row(lam))


_LOG_GAMMA = tuple(math.log(1.0 - 2.0 ** (-5.0 - h)) for h in range(D_HEADS))


def _per_head(lane_head, values):
    out = jnp.full(lane_head.shape, values[-1], F32)
    for h in range(len(values) - 2, -1, -1):
        out = jnp.where(lane_head == h, values[h], out)
    return out


def _retention_kernel(qk_ref, vg_ref, cos_ref, sin_ref, gw_ref, gb_ref, o_ref, s_ref, dm_ref):
    rows = qk_ref.shape[0]
    kw = qk_ref.shape[1] // 2
    vw = o_ref.shape[1]

    @pl.when((pl.program_id(0) == 0) & (pl.program_id(1) == 0))
    def _():
        rel = (_iota((rows, rows), 0) - _iota((rows, rows), 1)).astype(F32)
        for h in range(D_HEADS):
            dm_ref[h] = jnp.where(rel >= 0, jnp.exp(jnp.maximum(rel, 0.0) * _LOG_GAMMA[h]), 0.0)

    @pl.when(pl.program_id(1) == 0)
    def _():
        s_ref[...] = jnp.zeros(s_ref.shape, F32)

    def rotary(x):
        even = _mod(_iota((rows, LANES), 1), 2) == 0
        parts = []
        for c in range(kw // LANES):
            xc = x[:, c * LANES:(c + 1) * LANES]
            parts.append(jnp.where(even, pltpu.roll(xc, LANES - 1, 1), pltpu.roll(xc, 1, 1)))
        return x * cos_ref[...] + jnp.concatenate(parts, axis=1) * sin_ref[...]

    q = rotary(qk_ref[:, 0:kw])
    k = rotary(qk_ref[:, kw:2 * kw]) * (D_KEY_DIM ** -0.5)
    v = vg_ref[:, 0:vw]
    k_head = _div(_iota((1, kw), 1), D_KEY_DIM)
    lg_k = _per_head(k_head, _LOG_GAMMA)
    lg_v = _per_head(_div(_iota((1, vw), 1), D_VAL_DIM), _LOG_GAMMA)
    t_idx = _iota((rows, 1), 0).astype(F32)

    o_cross = _dot(q * jnp.exp((t_idx + 1.0) * lg_k), s_ref[...])
    inner = []
    for h in range(D_HEADS):
        scores = _dot(jnp.where(k_head == h, q, 0.0), k, NT) * dm_ref[h]
        inner.append(_dot(scores, v[:, h * D_VAL_DIM:(h + 1) * D_VAL_DIM]))
    o = o_cross + jnp.concatenate(inner, axis=1)

    k_dec = k * jnp.exp((rows - 1.0 - t_idx) * lg_k)
    same_head = _div(_iota((kw, vw), 0), D_KEY_DIM) == _div(_iota((kw, vw), 1), D_VAL_DIM)
    s_ref[...] = s_ref[...] * jnp.exp(float(rows) * lg_v) + jnp.where(same_head, _dot(k_dec, v, TN), 0.0)

    inv_n = 1.0 / D_VAL_DIM
    mu = _head_sums(o, D_VAL_DIM) * inv_n
    oc = o - mu
    var = _head_sums(oc * oc, D_VAL_DIM) * inv_n
    on = oc * lax.rsqrt(var + EPS) * gw_ref[...] + gb_ref[...]
    o_ref[...] = (_silu(vg_ref[:, vw:2 * vw]) * on).astype(o_ref.dtype)


def _retention(qk, vg, gn_w, gn_b, batch):
    m = qk.shape[0]
    kw = qk.shape[1] // 2
    vw = vg.shape[1] // 2
    rows = MIX_ROWS
    seq = m // batch
    tblocks = seq // rows
    inv = 1.0 / (ROPE_BASE ** jnp.linspace(0.0, 1.0, D_KEY_DIM // 2, dtype=F32))
    ang = jnp.arange(seq).astype(F32)[:, None] * inv[None, :]
    cos = jnp.tile(jnp.repeat(jnp.cos(ang), 2, axis=1), (1, kw // D_KEY_DIM))
    sin = jnp.tile(jnp.stack([-jnp.sin(ang), jnp.sin(ang)], axis=-1).reshape(seq, D_KEY_DIM),
                   (1, kw // D_KEY_DIM))
    row = lambda p: p.reshape(1, -1).astype(F32)
    row_spec = lambda w: pl.BlockSpec((rows, w), lambda b, i: (b * tblocks + i, 0))
    tab_spec = pl.BlockSpec((rows, kw), lambda b, i: (i, 0))
    return pl.pallas_call(
        _retention_kernel,
        grid=(batch, tblocks),
        in_specs=[row_spec(2 * kw), row_spec(2 * vw), tab_spec, tab_spec, _const_spec((1, vw)),
                  _const_spec((1, vw))],
        out_specs=row_spec(vw),
        out_shape=jax.ShapeDtypeStruct((m, vw), BF16),
        scratch_shapes=[pltpu.VMEM((kw, vw), F32), pltpu.VMEM((D_HEADS, rows, rows), F32)],
        compiler_params=_params(("arbitrary", "arbitrary")),
        name="retention_mixer",
    )(qk, vg, cos, sin, row(gn_w), row(gn_b))


def _ffn_kernel(x_ref, xh_ref, oa_ref, oah_ref, ob_ref, obh_ref, wo_ref, g_ref, wup_ref, wc_ref, wd_ref,
                *rest, seq_blocks, final):
    if final:
        gf_ref, y_ref, hc_ref, u_ref = rest
    else:
        y_ref, hc_ref, u_ref = rest
    rows = x_ref.shape[0]
    half = oa_ref.shape[1]
    ffn = wd_ref.shape[0]

    def mixed(x, oa, ob):
        return (x + jnp.dot(oa, wo_ref[0:half, :], preferred_element_type=F32)
                + jnp.dot(ob, wo_ref[half:2 * half, :], preferred_element_type=F32))

    x1 = mixed(x_ref[...], oa_ref[...], ob_ref[...])
    x1h = mixed(xh_ref[...], oah_ref[...], obh_ref[...])
    starts_seq = (pl.program_id(0) % seq_blocks) == 0
    hc_ref[0:BF16_ROWS, :] = (_rms(x1h, g_ref[...]) * jnp.where(starts_seq, 0.0, 1.0)).astype(BF16)
    hc_ref[BF16_ROWS:BF16_ROWS + rows, :] = _rms(x1, g_ref[...]).astype(BF16)

    u_ref[...] = jnp.dot(hc_ref[...], wup_ref[:, 0:ffn], preferred_element_type=F32)
    v = jnp.dot(hc_ref[BF16_ROWS:BF16_ROWS + rows, :], wup_ref[:, ffn:2 * ffn], preferred_element_type=F32)
    uc = wc_ref[FFN_CONV - 1:FFN_CONV, :] * u_ref[pl.ds(BF16_ROWS, rows), :]
    for back in range(1, FFN_CONV):
        uc = uc + wc_ref[FFN_CONV - 1 - back:FFN_CONV - back, :] * u_ref[pl.ds(BF16_ROWS - back, rows), :]
    gl = (_silu(uc) * v).astype(BF16)
    y = x1 + jnp.dot(gl, wd_ref[...], preferred_element_type=F32)
    if final:
        y = _rms(y, gf_ref[...])
    y_ref[...] = y


def _ffn(x, oa, ob, w_out, norm_w, w_up, w_conv, w_down, seq, final_norm=None):
    m, d = x.shape
    half = oa.shape[1]
    ffn = w_down.shape[0]
    tm = FFN_ROWS
    halo = BF16_ROWS
    final = final_norm is not None
    row_spec = lambda w: pl.BlockSpec((tm, w), lambda i: (i, 0))
    halo_spec = lambda w: pl.BlockSpec((halo, w), lambda i: (jnp.maximum(i * (tm // halo) - 1, 0), 0))
    resident = lambda w: pl.BlockSpec(w.shape, lambda i: (0, 0), pipeline_mode=pl.Buffered(1))
    row = lambda p: p.reshape(1, -1).astype(F32)
    in_specs = [row_spec(d), halo_spec(d), row_spec(half), halo_spec(half), row_spec(half), halo_spec(half),
                resident(w_out), _const_spec((1, d)), resident(w_up), _const_spec(w_conv.shape), resident(w_down)]
    args = [x, x, oa, oa, ob, ob, w_out, row(norm_w), w_up, w_conv, w_down]
    if final:
        in_specs.append(_const_spec((1, d)))
        args.append(row(final_norm))
    return pl.pallas_call(
        functools.partial(_ffn_kernel, seq_blocks=seq // tm, final=final),
        grid=(m // tm,),
        in_specs=in_specs,
        out_specs=row_spec(d),
        out_shape=jax.ShapeDtypeStruct((m, d), F32),
        scratch_shapes=[pltpu.VMEM((tm + halo, d), BF16), pltpu.VMEM((tm + halo, ffn), F32)],
        compiler_params=_params(("parallel",)),
        name="outproj_ffn_final" if final else "outproj_ffn",
    )(*args)


def kernel(x, l0_norm1, l0_w_in, l0_a_conv, l0_a_A_log, l0_a_dt_bias, l0_a_norm, l0_b_mu, l0_b_w0, l0_b_w2, l0_b_a0, l0_b_a2, l0_b_g2, l0_b_k_k, l0_b_k_a, l0_b_r_k, l0_b_ln_w, l0_b_ln_b, l0_w_out, l0_norm2, l0_ffn_up, l0_ffn_conv, l0_ffn_down, l1_norm1, l1_w_in, l1_c_conv_w, l1_c_conv_b, l1_c_wa, l1_c_ba, l1_c_wx, l1_c_bx, l1_c_lambda, l1_d_gn_w, l1_d_gn_b, l1_w_out, l1_norm2, l1_ffn_up, l1_ffn_conv, l1_ffn_down, final_norm):
    batch, seq, d = x.shape
    xf = x.reshape(batch * seq, d)
    bf = lambda w: w.astype(BF16)

    a_w = l0_a_conv.shape[1] // 3
    a_heads = l0_a_A_log.shape[0]
    a_in = 4 * a_w + 2 * a_heads
    b_in = l0_w_in.shape[1] - a_in
    w_gates = jnp.concatenate([bf(l0_w_in[:, 4 * a_w:a_in]), jnp.zeros((d, LANES - 2 * a_heads), BF16)], axis=1)
    qkvz, ba, pb = _norm_proj(xf, l0_norm1, [bf(l0_w_in[:, 0:4 * a_w]), w_gates, bf(l0_w_in[:, a_in:])],
                              ((4 * a_w,), (LANES,), (b_in,)), "norm_proj0")
    o_a = _gdn(qkvz, ba, l0_a_conv, l0_a_A_log, l0_a_dt_bias, l0_a_norm, batch)
    o_bb = _rwkv(pb, l0_b_mu, l0_b_w0, bf(l0_b_w2), l0_b_a0, bf(l0_b_a2), bf(l0_b_g2),
                 l0_b_k_k, l0_b_k_a, l0_b_r_k, l0_b_ln_w, l0_b_ln_b, batch)
    x1 = _ffn(xf, o_a, o_bb, bf(l0_w_out), l0_norm2, bf(l0_ffn_up), l0_ffn_conv, bf(l0_ffn_down), seq)

    c_w = l1_c_lambda.shape[0]
    d_w = l1_d_gn_w.shape[0]
    qk_w = l1_w_in.shape[1] - 2 * c_w - 2 * d_w
    gx, qk, vg = _norm_proj(x1, l1_norm1, [bf(l1_w_in)], ((2 * c_w, qk_w, 2 * d_w),), "norm_proj1")
    o_c = _rglru(gx, l1_c_conv_w, l1_c_conv_b, l1_c_wa, l1_c_ba, l1_c_wx, l1_c_bx, l1_c_lambda, batch)
    o_d = _retention(qk, vg, l1_d_gn_w, l1_d_gn_b, batch)
    y = _ffn(x1, o_c, o_d, bf(l1_w_out), l1_norm2, bf(l1_ffn_up), l1_ffn_conv, bf(l1_ffn_down), seq,
             final_norm=final_norm)
    return y.reshape(batch, seq, d)
```

```python
import functools
import math

import numpy as np
import jax
import jax.numpy as jnp
from jax import lax
from jax.experimental import pallas as pl
from jax.experimental.pallas import tpu as pltpu

F32 = jnp.float32
BF16 = jnp.bfloat16

EPS = 1e-6
CHUNK = 64
A_HEAD_DIM = 128
A_CONV = 4
B_HEAD_DIM = 64
B_LN_EPS = 64e-5
C_GATE_SCALE = 8.0
C_CONV = 4
D_KEY_DIM = 64
D_VAL_DIM = 128
D_HEADS = 4
ROPE_BASE = 10000.0
FFN_CONV = 3

LANES = 128
SUBLANES = 8
BF16_ROWS = 16
VMEM_LIMIT = 56 * 1024 * 1024

MIX_ROWS = 256
PROJ_ROWS = 256
FFN_ROWS = 512

NN = (((1,), (0,)), ((), ()))
NT = (((1,), (1,)), ((), ()))
TN = (((0,), (0,)), ((), ()))


def _pieces(x, n):
    if x.dtype == BF16:
        return [x]
    out, r = [], x
    for i in range(n):
        p = r.astype(BF16)
        out.append(p)
        if i + 1 < n:
            r = r - p.astype(F32)
    return out


def _dot(a, b, dims=NN, pa=1, pb=1):
    ap, bp = _pieces(a, pa), _pieces(b, pb)
    order = max(len(ap), len(bp))
    acc = None
    for i in reversed(range(len(ap))):
        for j in reversed(range(len(bp))):
            if i + j < order:
                t = lax.dot_general(ap[i], bp[j], dims, preferred_element_type=F32)
                acc = t if acc is None else acc + t
    return acc


def _sigmoid(x):
    return 1.0 / (1.0 + jnp.exp(-x))


def _silu(x):
    return x * _sigmoid(x)


def _softplus(x):
    return jnp.maximum(x, 0.0) + jnp.log1p(jnp.exp(-jnp.abs(x)))


def _gelu_tanh(x):
    return 0.5 * x * (1.0 + jnp.tanh(math.sqrt(2.0 / math.pi) * (x + 0.044715 * (x * x * x))))


def _rms(x, g):
    return x * lax.rsqrt(jnp.mean(x * x, axis=-1, keepdims=True) + EPS) * g


def _iota(shape, dim):
    return lax.broadcasted_iota(jnp.int32, shape, dim)


def _div(i, n):
    assert n & (n - 1) == 0
    return i >> (n.bit_length() - 1)


def _mod(i, n):
    assert n & (n - 1) == 0
    return i & (n - 1)


def _pair_masks():
    t, s = _iota((CHUNK, 2 * CHUNK), 0), _mod(_iota((CHUNK, 2 * CHUNK), 1), CHUNK)
    return s < t, s <= t


def _stack_heads(x, n):
    head = _div(_iota(x.shape, 1), x.shape[1] // n)
    return jnp.concatenate([jnp.where(head == h, x, 0.0) for h in range(n)], axis=0)


def _stack_pair(x):
    return _stack_heads(x, 2)


def _inv_unit_lower(n_mats):
    shape = n_mats[0].shape
    heads = shape[1] // CHUNK
    stack = functools.partial(_stack_heads, n=heads)
    eye = (_mod(_iota(shape, 1), CHUNK) == _iota(shape, 0)).astype(F32)
    ts = [eye + n for n in n_mats]
    ps = [_dot(n, stack(n)) for n in n_mats]
    levels = int(math.log2(CHUNK)) - 1
    for level in range(levels):
        last = level == levels - 1
        lhs = ts if last else [jnp.concatenate([t, p], axis=0) for t, p in zip(ts, ps)]
        prod = [_dot(l, stack(p)) for l, p in zip(lhs, ps)]
        ts = [t + pr[0:CHUNK] for t, pr in zip(ts, prod)]
        if not last:
            ps = [pr[CHUNK:2 * CHUNK] for pr in prod]
    return ts


def _head_sums(x, dim):
    assert dim in (LANES, LANES // 2)
    low = _iota((x.shape[0], LANES), 1) < dim
    parts = []
    for c in range(x.shape[1] // LANES):
        xc = x[:, c * LANES:(c + 1) * LANES]
        total = jnp.sum(xc, axis=-1, keepdims=True)
        if dim == LANES:
            parts.append(jnp.broadcast_to(total, xc.shape))
        else:
            first = jnp.sum(jnp.where(low, xc, 0.0), axis=-1, keepdims=True)
            parts.append(jnp.where(low, first, total - first))
    return jnp.concatenate(parts, axis=1)


def _head_expand(cols, first, heads, dim):
    return jnp.concatenate([jnp.broadcast_to(cols[:, first + h:first + h + 1], (cols.shape[0], dim))
                            for h in range(heads)], axis=1)


def _shifted_rows(xp_ref, back, rows):
    if back == 0:
        return xp_ref[SUBLANES:SUBLANES + rows, :]
    return pltpu.roll(xp_ref[...], back, 0)[SUBLANES:SUBLANES + rows, :]


def _params(sem):
    return pltpu.CompilerParams(dimension_semantics=sem, vmem_limit_bytes=VMEM_LIMIT)


def _const_spec(shape):
    return pl.BlockSpec(shape, lambda *_: (0,) * len(shape))


def _chunk_sum_matrix(rows):
    t = np.arange(rows)
    same = (t[:, None] // CHUNK) == (t[None, :] // CHUNK)
    return jnp.asarray(np.concatenate([same & (t[None, :] <= t[:, None]), same], axis=0), BF16)


def _norm_proj_kernel(x_ref, g_ref, *refs, splits):
    w_refs, o_refs = refs[:len(splits)], refs[len(splits):]
    h = _rms(x_ref[...], g_ref[...]).astype(BF16)
    outs = iter(o_refs)
    for w_ref, cols in zip(w_refs, splits):
        off = 0
        for n in cols:
            next(outs)[...] = jnp.dot(h, w_ref[:, off:off + n], preferred_element_type=F32)
            off += n


def _norm_proj(x, g, ws, splits, name):
    m, d = x.shape
    tm = PROJ_ROWS
    widths = [n for cols in splits for n in cols]
    return pl.pallas_call(
        functools.partial(_norm_proj_kernel, splits=splits),
        grid=(m // tm,),
        in_specs=[pl.BlockSpec((tm, d), lambda i: (i, 0)), _const_spec((1, d))] + [_const_spec(w.shape) for w in ws],
        out_specs=[pl.BlockSpec((tm, n), lambda i: (i, 0)) for n in widths],
        out_shape=[jax.ShapeDtypeStruct((m, n), F32) for n in widths],
        compiler_params=_params(("parallel",)),
        name=name,
    )(x, g.reshape(1, d), *ws)


def _gdn_kernel(qkvz_ref, ba_ref, cw_ref, alog_ref, dtb_ref, nw_ref, lcat_ref,
                o_ref, xp_ref, s_ref, qs_ref, k_ref, kb_ref, vb_ref, kbg_ref, qd_ref, gc_ref,
                gl_ref, us_ref, wq_ref, qkm_ref, kdt_ref):
    rows = qkvz_ref.shape[0]
    width = o_ref.shape[1]
    tb = pl.program_id(1)

    @pl.when(tb == 0)
    def _():
        xp_ref[0:SUBLANES, :] = jnp.zeros((SUBLANES, xp_ref.shape[1]), F32)
        s_ref[...] = jnp.zeros(s_ref.shape, F32)

    xp_ref[SUBLANES:SUBLANES + rows, :] = qkvz_ref[:, 0:3 * width]
    acc = cw_ref[A_CONV - 1:A_CONV, :] * _shifted_rows(xp_ref, 0, rows)
    for back in range(1, A_CONV):
        acc = acc + cw_ref[A_CONV - 1 - back:A_CONV - back, :] * _shifted_rows(xp_ref, back, rows)
    xp_ref[0:SUBLANES, :] = xp_ref[rows:rows + SUBLANES, :]
    act = _silu(acc)
    q, k, v = act[:, 0:width], act[:, width:2 * width], act[:, 2 * width:3 * width]

    def l2n(x):
        return x * lax.rsqrt(_head_sums(x * x, A_HEAD_DIM) + EPS)

    ba = ba_ref[...]
    n_heads = width // A_HEAD_DIM
    bg = jnp.where(_iota(ba.shape, 1) < n_heads, _sigmoid(ba),
                   -jnp.exp(alog_ref[...]) * _softplus(ba + dtb_ref[...]))
    ct = _dot(lcat_ref[...], bg, pb=3)
    beta_f = _head_expand(bg, 0, n_heads, A_HEAD_DIM)
    gc_f = _head_expand(ct[0:rows], n_heads, n_heads, A_HEAD_DIM)
    gl_f = _head_expand(ct[rows:2 * rows], n_heads, n_heads, A_HEAD_DIM)

    kn = l2n(k)
    qs = l2n(q) * (A_HEAD_DIM ** -0.5)
    kb = kn * beta_f
    e_gc = jnp.exp(gc_f)
    qs_ref[...] = qs
    k_ref[...] = kn
    kb_ref[...] = kb
    vb_ref[...] = v * beta_f
    kbg_ref[...] = kb * e_gc
    qd_ref[...] = qs * e_gc
    gc_ref[...] = gc_f
    gl_ref[...] = gl_f

    pair_w = 2 * A_HEAD_DIM
    assert A_HEAD_DIM == 2 * CHUNK
    n_pairs = width // pair_w
    n_chunks = rows // CHUNK
    strict, lower = _pair_masks()
    first_head = _iota((CHUNK, 2 * CHUNK), 1) < CHUNK
    same_head = (_iota((pair_w, pair_w), 0) < A_HEAD_DIM) == (_iota((pair_w, pair_w), 1) < A_HEAD_DIM)

    def ld(ref, c, p):
        return ref[c * CHUNK:(c + 1) * CHUNK, p * pair_w:(p + 1) * pair_w]

    keys = [(c, p) for c in range(n_chunks) for p in range(n_pairs)]
    g = [_dot(jnp.concatenate([ld(kb_ref, *k), ld(qs_ref, *k)], axis=0), _stack_pair(ld(k_ref, *k)), NT)
         for k in keys]
    n_ms = []
    for i, k in enumerate(keys):
        gcp = ld(gc_ref, *k)
        g_t = jnp.where(first_head, gcp[:, 0:A_HEAD_DIM], gcp[:, A_HEAD_DIM:pair_w])
        g_s = jnp.concatenate([gcp[:, 0:A_HEAD_DIM], gcp[:, A_HEAD_DIM:pair_w]], axis=0).T[0:CHUNK]
        dec = jnp.exp(jnp.where(lower, g_t - g_s, 0.0))
        n_ms.append(jnp.where(strict, -(g[i][0:CHUNK] * dec), 0.0))
        qkm_ref[i] = jnp.where(lower, g[i][CHUNK:2 * CHUNK] * dec, 0.0).astype(BF16)
    t_ms = _inv_unit_lower(n_ms)
    for i, k in enumerate(keys):
        uw = _dot(t_ms[i], jnp.concatenate([_stack_pair(ld(vb_ref, *k)), _stack_pair(ld(kbg_ref, *k))], axis=1))
        us_ref[i] = uw[:, 0:pair_w]
        wq_ref[i] = jnp.concatenate([uw[:, pair_w:2 * pair_w], ld(qd_ref, *k)], axis=0).astype(BF16)
        kdt_ref[i] = (ld(k_ref, *k) * jnp.exp(ld(gl_ref, *k) - ld(gc_ref, *k))).T.astype(BF16)

    s = [s_ref[p] for p in range(n_pairs)]
    for c in range(n_chunks):
        idx = [c * n_pairs + p for p in range(n_pairs)]
        x = [_dot(wq_ref[i], s[p]) for p, i in enumerate(idx)]
        v_new = [us_ref[i] - x[p][0:CHUNK] for p, i in enumerate(idx)]
        o_c = [x[p][CHUNK:2 * CHUNK] + _dot(qkm_ref[i], _stack_pair(v_new[p])) for p, i in enumerate(idx)]
        for p, i in enumerate(idx):
            g_last = jnp.exp(gl_ref[c * CHUNK:c * CHUNK + 1, p * pair_w:(p + 1) * pair_w])
            s[p] = s[p] * g_last + jnp.where(same_head, _dot(kdt_ref[i], v_new[p]), 0.0)
        o = jnp.concatenate(o_c, axis=1)
        ms = _head_sums(o * o, A_HEAD_DIM) * (1.0 / A_HEAD_DIM)
        z = qkvz_ref[c * CHUNK:(c + 1) * CHUNK, 3 * width:4 * width]
        o_ref[c * CHUNK:(c + 1) * CHUNK, :] = (o * lax.rsqrt(ms + EPS) * nw_ref[...] * _silu(z)).astype(o_ref.dtype)
    for p in range(n_pairs):
        s_ref[p] = s[p]


def _gdn(qkvz, ba, conv_w, a_log, dt_bias, norm_w, batch):
    m = qkvz.shape[0]
    width = qkvz.shape[1] // 4
    n_heads = width // A_HEAD_DIM
    rows = MIX_ROWS
    tblocks = m // batch // rows
    pad = lambda p: jnp.zeros((1, LANES), F32).at[0, n_heads:2 * n_heads].set(p)
    lcat = _chunk_sum_matrix(rows)
    row_spec = lambda w: pl.BlockSpec((rows, w), lambda b, i: (b * tblocks + i, 0))
    act = lambda: pltpu.VMEM((rows, width), F32)
    pair_w = 2 * A_HEAD_DIM
    n_keys = (rows // CHUNK) * (width // pair_w)
    return pl.pallas_call(
        _gdn_kernel,
        grid=(batch, tblocks),
        in_specs=[row_spec(4 * width), row_spec(LANES), _const_spec(conv_w.shape), _const_spec((1, LANES)),
                  _const_spec((1, LANES)), _const_spec((1, width)), _const_spec(lcat.shape)],
        out_specs=row_spec(width),
        out_shape=jax.ShapeDtypeStruct((m, width), BF16),
        scratch_shapes=[pltpu.VMEM((rows + SUBLANES, 3 * width), F32),
                        pltpu.VMEM((width // pair_w, pair_w, pair_w), F32)]
                       + [act() for _ in range(8)]
                       + [pltpu.VMEM((n_keys, CHUNK, pair_w), F32), pltpu.VMEM((n_keys, 2 * CHUNK, pair_w), BF16),
                          pltpu.VMEM((n_keys, CHUNK, 2 * CHUNK), BF16), pltpu.VMEM((n_keys, pair_w, CHUNK), BF16)],
        compiler_params=_params(("arbitrary", "arbitrary")),
        name="gdn_mixer",
    )(qkvz, ba, conv_w, pad(a_log), pad(dt_bias), jnp.tile(norm_w, n_heads).reshape(1, width), lcat)


def _rwkv_kernel(pb_ref, mu_ref, w0_ref, w2_ref, a0_ref, a2_ref, g2_ref, kk_ref, ka_ref, rk_ref, lnw_ref,
                 lnb_ref, lcat_ref, o_ref, xp_ref, s_ref, at_ref, rt_ref, bt_ref, kt_ref,
                 v_ref, bv_ref, km_ref, cum_ref, tot_ref, r_ref, gate_ref, lx_ref, tw_ref, mc_ref, bkt_ref,
                 pcol_ref):
    rows = pb_ref.shape[0]
    width = o_ref.shape[1]
    tb = pl.program_id(1)

    @pl.when(tb == 0)
    def _():
        xp_ref[0:SUBLANES, :] = jnp.zeros((SUBLANES, xp_ref.shape[1]), F32)
        s_ref[...] = jnp.zeros(s_ref.shape, F32)

    p = pb_ref[...]
    xp_ref[SUBLANES:SUBLANES + rows, :] = p
    prev = _shifted_rows(xp_ref, 1, rows)
    xp_ref[0:SUBLANES, :] = xp_ref[rows:rows + SUBLANES, :]
    m = p + (prev - p) * mu_ref[...]
    r, kr, vr = m[:, 0:width], m[:, width:2 * width], m[:, 2 * width:3 * width]
    o1 = 3 * width
    o2 = o1 + w2_ref.shape[0]
    o3 = o2 + a2_ref.shape[0]
    w_lo, a_lo, g_lo = m[:, o1:o2], m[:, o2:o3], m[:, o3:]

    lw = -math.exp(-0.5) * _sigmoid(w0_ref[...] + _dot(jnp.tanh(w_lo), w2_ref[...]))
    a_lr = _sigmoid(a0_ref[...] + _dot(a_lo, a2_ref[...]))
    gate_ref[...] = _dot(_sigmoid(g_lo), g2_ref[...])

    def gsum(x):
        return _head_sums(x, B_HEAD_DIM)

    kk = kr * kk_ref[...]
    k_mod = kr * (1.0 + (a_lr - 1.0) * ka_ref[...])
    kk = kk * lax.rsqrt(gsum(kk * kk) + EPS)
    a_vec = -kk
    b_vec = kk * a_lr

    ct = _dot(lcat_ref[...], lw, pb=2)
    cum, tot = ct[0:rows], ct[rows:2 * rows]
    e_neg = jnp.exp(-cum)
    rt_ref[...] = r * jnp.exp(cum)
    at_ref[...] = a_vec * jnp.exp(cum - lw)
    bt_ref[...] = b_vec * e_neg
    kt_ref[...] = k_mod * e_neg
    bv_ref[...] = b_vec
    km_ref[...] = k_mod
    cum_ref[...] = cum
    tot_ref[...] = tot
    r_ref[...] = r
    v_ref[...] = vr

    pair_w = 2 * B_HEAD_DIM
    n_pairs = width // pair_w
    n_chunks = rows // CHUNK
    strict, lower = _pair_masks()
    lower2 = jnp.concatenate([lower, lower], axis=1)
    same_head = (_iota((pair_w, pair_w), 0) < B_HEAD_DIM) == (_iota((pair_w, pair_w), 1) < B_HEAD_DIM)

    def ld(ref, c, pr):
        return ref[c * CHUNK:(c + 1) * CHUNK, pr * pair_w:(pr + 1) * pair_w]

    keys = [(c, pr) for c in range(n_chunks) for pr in range(n_pairs)]
    g = [_dot(jnp.concatenate([ld(at_ref, *k), ld(rt_ref, *k)], axis=0),
              jnp.concatenate([_stack_pair(ld(bt_ref, *k)), _stack_pair(ld(kt_ref, *k))], axis=0), NT)
         for k in keys]
    t_ms = _inv_unit_lower([jnp.where(strict, gi[0:CHUNK, 0:2 * CHUNK], 0.0) for gi in g])
    w1 = [_dot(jnp.where(strict, g[i][0:CHUNK, 2 * CHUNK:4 * CHUNK], 0.0), _stack_pair(ld(v_ref, *k)))
          for i, k in enumerate(keys)]
    taw = [_dot(t_ms[i], jnp.concatenate([_stack_pair(ld(at_ref, *k)), _stack_pair(w1[i])], axis=1))
           for i, k in enumerate(keys)]
    for i, k in enumerate(keys):
        lx_ref[i] = jnp.concatenate([taw[i][:, 0:pair_w], ld(rt_ref, *k)], axis=0).astype(BF16)
        tw_ref[i] = taw[i][:, pair_w:2 * pair_w]
        mc_ref[i] = jnp.where(lower2, g[i][CHUNK:2 * CHUNK], 0.0).astype(BF16)
        tot_c = ld(tot_ref, *k)
        e_dec = jnp.exp(tot_c - ld(cum_ref, *k))
        bkt_ref[i] = jnp.concatenate([ld(bv_ref, *k) * e_dec, ld(km_ref, *k) * e_dec], axis=0).T.astype(BF16)
        pcol_ref[i] = jnp.broadcast_to(jnp.exp(tot_c[0:SUBLANES].T[:, 0:1]), (pair_w, pair_w))

    s = [s_ref[pr] for pr in range(n_pairs)]
    for c in range(n_chunks):
        idx = [c * n_pairs + pr for pr in range(n_pairs)]
        x = [_dot(lx_ref[j], s[pr]) for pr, j in enumerate(idx)]
        u = [x[pr][0:CHUNK] + tw_ref[j] for pr, j in enumerate(idx)]
        y_c = [x[pr][CHUNK:2 * CHUNK]
               + _dot(mc_ref[j], jnp.concatenate([_stack_pair(u[pr]), _stack_pair(ld(v_ref, c, pr))], axis=0))
               for pr, j in enumerate(idx)]
        for pr, j in enumerate(idx):
            upd = _dot(bkt_ref[j], jnp.concatenate([u[pr], ld(v_ref, c, pr)], axis=0))
            s[pr] = s[pr] * pcol_ref[j] + jnp.where(same_head, upd, 0.0)
        rows_c = slice(c * CHUNK, (c + 1) * CHUNK)
        y = jnp.concatenate(y_c, axis=1)
        inv_n = 1.0 / B_HEAD_DIM
        mu = gsum(y) * inv_n
        yc = y - mu
        var = gsum(yc * yc) * inv_n
        yn = yc * lax.rsqrt(var + B_LN_EPS) * lnw_ref[...] + lnb_ref[...]
        bonus = gsum(r_ref[rows_c, :] * km_ref[rows_c, :] * rk_ref[...]) * v_ref[rows_c, :]
        o_ref[rows_c, :] = ((yn + bonus) * gate_ref[rows_c, :]).astype(o_ref.dtype)
    for pr in range(n_pairs):
        s_ref[pr] = s[pr]


def _rwkv(pb, mu, w0, w2, a0, a2, g2, k_k, k_a, r_k, ln_w, ln_b, batch):
    m, in_w = pb.shape
    width = w0.shape[0]
    rows = MIX_ROWS
    tblocks = m // batch // rows
    lcat = _chunk_sum_matrix(rows)
    row = lambda p: p.reshape(1, -1).astype(F32)
    row_spec = lambda w: pl.BlockSpec((rows, w), lambda b, i: (b * tblocks + i, 0))
    vec = _const_spec((1, width))
    act = lambda: pltpu.VMEM((rows, width), F32)
    pair_w = 2 * B_HEAD_DIM
    n_keys = (rows // CHUNK) * (width // pair_w)
    return pl.pallas_call(
        _rwkv_kernel,
        grid=(batch, tblocks),
        in_specs=[row_spec(in_w), _const_spec((1, in_w)), vec, _const_spec(w2.shape), vec,
                  _const_spec(a2.shape), _const_spec(g2.shape), vec, vec, vec, vec, vec,
                  _const_spec(lcat.shape)],
        out_specs=row_spec(width),
        out_shape=jax.ShapeDtypeStruct((m, width), BF16),
        scratch_shapes=[pltpu.VMEM((rows + SUBLANES, in_w), F32),
                        pltpu.VMEM((width // pair_w, pair_w, pair_w), F32)]
                       + [act() for _ in range(11)]
                       + [pltpu.VMEM((n_keys, 2 * CHUNK, pair_w), BF16), pltpu.VMEM((n_keys, CHUNK, pair_w), F32),
                          pltpu.VMEM((n_keys, CHUNK, 4 * CHUNK), BF16), pltpu.VMEM((n_keys, pair_w, 2 * CHUNK), BF16),
                          pltpu.VMEM((n_keys, pair_w, pair_w), F32)],
        compiler_params=_params(("arbitrary", "arbitrary")),
        name="rwkv_mixer",
    )(pb, row(mu), row(w0), w2, row(a0), a2, g2, row(k_k), row(k_a), row(r_k), row(ln_w), row(ln_b),
      lcat)


def _rglru_body(gx_ref, cw_ref, cb_ref, wa_ref, ba_ref, wx_ref, bx_ref, lam_ref, o_ref, xp_ref, h_ref):
    rows = gx_ref.shape[0]
    width = o_ref.shape[1]
    xp_ref[SUBLANES:SUBLANES + rows, :] = gx_ref[:, width:2 * width]
    xc = cb_ref[...] + cw_ref[C_CONV - 1:C_CONV, :] * _shifted_rows(xp_ref, 0, rows)
    for back in range(1, C_CONV):
        xc = xc + cw_ref[C_CONV - 1 - back:C_CONV - back, :] * _shifted_rows(xp_ref, back, rows)
    xp_ref[0:SUBLANES, :] = xp_ref[rows:rows + SUBLANES, :]

    gate_r = _sigmoid(_dot(xc, wa_ref[...]) + ba_ref[...])
    gate_i = _sigmoid(_dot(xc, wx_ref[...]) + bx_ref[...])
    log_a = -C_GATE_SCALE * gate_r * _softplus(-lam_ref[...])
    a = jnp.exp(log_a)
    d = jnp.sqrt(jnp.tanh(-log_a) * (a * a + 1.0)) * (gate_i * xc)

    t_idx = _iota((rows, width), 0)
    shift = 1
    while shift < SUBLANES:
        keep = t_idx >= shift
        d = jnp.where(keep, a * pltpu.roll(d, shift, 0) + d, d)
        a = jnp.where(keep, a * pltpu.roll(a, shift, 0), a)
        shift *= 2
    while shift < rows:
        d = jnp.concatenate([d[:shift], a[shift:] * d[:rows - shift] + d[shift:]], axis=0)
        a = jnp.concatenate([a[:shift], a[shift:] * a[:rows - shift]], axis=0)
        shift *= 2
    h = a * h_ref[0:1, :] + d
    h_ref[...] = jnp.broadcast_to(h[rows - 1:rows, :], h_ref.shape)
    o_ref[...] = (h * _gelu_tanh(gx_ref[:, 0:width])).astype(o_ref.dtype)


_LOG_GAMMA = tuple(math.log(1.0 - 2.0 ** (-5.0 - h)) for h in range(D_HEADS))


def _per_head(lane_head, values):
    out = jnp.full(lane_head.shape, values[-1], F32)
    for h in range(len(values) - 2, -1, -1):
        out = jnp.where(lane_head == h, values[h], out)
    return out


def _retention_body(qk_ref, vg_ref, cos_ref, sin_ref, gw_ref, gb_ref, o_ref, s_ref, dm_ref):
    rows = qk_ref.shape[0]
    kw = qk_ref.shape[1] // 2
    vw = o_ref.shape[1]

    def rotary(x):
        even = _mod(_iota((rows, LANES), 1), 2) == 0
        parts = []
        for c in range(kw // LANES):
            xc = x[:, c * LANES:(c + 1) * LANES]
            parts.append(jnp.where(even, pltpu.roll(xc, LANES - 1, 1), pltpu.roll(xc, 1, 1)))
        return x * cos_ref[...] + jnp.concatenate(parts, axis=1) * sin_ref[...]

    q = rotary(qk_ref[:, 0:kw])
    k = rotary(qk_ref[:, kw:2 * kw]) * (D_KEY_DIM ** -0.5)
    v = vg_ref[:, 0:vw]
    k_head = _div(_iota((1, kw), 1), D_KEY_DIM)
    lg_k = _per_head(k_head, _LOG_GAMMA)
    lg_v = _per_head(_div(_iota((1, vw), 1), D_VAL_DIM), _LOG_GAMMA)
    t_idx = _iota((rows, 1), 0).astype(F32)

    o_cross = _dot(q * jnp.exp((t_idx + 1.0) * lg_k), s_ref[...])
    inner = []
    for h in range(D_HEADS):
        scores = _dot(jnp.where(k_head == h, q, 0.0), k, NT) * dm_ref[h]
        inner.append(_dot(scores, v[:, h * D_VAL_DIM:(h + 1) * D_VAL_DIM]))
    o = o_cross + jnp.concatenate(inner, axis=1)

    k_dec = k * jnp.exp((rows - 1.0 - t_idx) * lg_k)
    same_head = _div(_iota((kw, vw), 0), D_KEY_DIM) == _div(_iota((kw, vw), 1), D_VAL_DIM)
    s_ref[...] = s_ref[...] * jnp.exp(float(rows) * lg_v) + jnp.where(same_head, _dot(k_dec, v, TN), 0.0)

    inv_n = 1.0 / D_VAL_DIM
    mu = _head_sums(o, D_VAL_DIM) * inv_n
    oc = o - mu
    var = _head_sums(oc * oc, D_VAL_DIM) * inv_n
    on = oc * lax.rsqrt(var + EPS) * gw_ref[...] + gb_ref[...]
    o_ref[...] = (_silu(vg_ref[:, vw:2 * vw]) * on).astype(o_ref.dtype)


def _mixer_cd_kernel(gx_ref, cw_ref, cb_ref, wa_ref, ba_ref, wx_ref, bx_ref, lam_ref, qk_ref, vg_ref, cos_ref,
                     sin_ref, gw_ref, gb_ref, oc_ref, od_ref, xp_ref, h_ref, s_ref, dm_ref):
    rows = qk_ref.shape[0]

    @pl.when((pl.program_id(0) == 0) & (pl.program_id(1) == 0))
    def _():
        rel = (_iota((rows, rows), 0) - _iota((rows, rows), 1)).astype(F32)
        for h in range(D_HEADS):
            dm_ref[h] = jnp.where(rel >= 0, jnp.exp(jnp.maximum(rel, 0.0) * _LOG_GAMMA[h]), 0.0)

    @pl.when(pl.program_id(1) == 0)
    def _():
        xp_ref[0:SUBLANES, :] = jnp.zeros((SUBLANES, xp_ref.shape[1]), F32)
        h_ref[...] = jnp.zeros(h_ref.shape, F32)
        s_ref[...] = jnp.zeros(s_ref.shape, F32)

    _retention_body(qk_ref, vg_ref, cos_ref, sin_ref, gw_ref, gb_ref, od_ref, s_ref, dm_ref)
    _rglru_body(gx_ref, cw_ref, cb_ref, wa_ref, ba_ref, wx_ref, bx_ref, lam_ref, oc_ref, xp_ref, h_ref)


def _mixer_cd(gx, qk, vg, conv_w, conv_b, wa, ba, wx, bx, lam, gn_w, gn_b, batch):
    m = qk.shape[0]
    cw = gx.shape[1] // 2
    kw = qk.shape[1] // 2
    vw = vg.shape[1] // 2
    rows = MIX_ROWS
    seq = m // batch
    tblocks = seq // rows
    inv = 1.0 / (ROPE_BASE ** jnp.linspace(0.0, 1.0, D_KEY_DIM // 2, dtype=F32))
    ang = jnp.arange(seq).astype(F32)[:, None] * inv[None, :]
    cos = jnp.tile(jnp.repeat(jnp.cos(ang), 2, axis=1), (1, kw // D_KEY_DIM))
    sin = jnp.tile(jnp.stack([-jnp.sin(ang), jnp.sin(ang)], axis=-1).reshape(seq, D_KEY_DIM),
                   (1, kw // D_KEY_DIM))
    row = lambda p: p.reshape(1, -1).astype(F32)
    blockdiag = lambda w: jax.scipy.linalg.block_diag(*[w[i] for i in range(w.shape[0])]).astype(BF16)
    row_spec = lambda w: pl.BlockSpec((rows, w), lambda b, i: (b * tblocks + i, 0))
    tab_spec = pl.BlockSpec((rows, kw), lambda b, i: (i, 0))
    cvec, cmat, vvec = _const_spec((1, cw)), _const_spec((cw, cw)), _const_spec((1, vw))
    return pl.pallas_call(
        _mixer_cd_kernel,
        grid=(batch, tblocks),
        in_specs=[row_spec(2 * cw), _const_spec(conv_w.shape), cvec, cmat, cvec, cmat, cvec, cvec,
                  row_spec(2 * kw), row_spec(2 * vw), tab_spec, tab_spec, vvec, vvec],
        out_specs=[row_spec(cw), row_spec(vw)],
        out_shape=[jax.ShapeDtypeStruct((m, cw), BF16), jax.ShapeDtypeStruct((m, vw), BF16)],
        scratch_shapes=[pltpu.VMEM((rows + SUBLANES, cw), F32), pltpu.VMEM((SUBLANES, cw), F32),
                        pltpu.VMEM((kw, vw), F32), pltpu.VMEM((D_HEADS, rows, rows), F32)],
        compiler_params=_params(("arbitrary", "arbitrary")),
        name="rglru_retention_mixer",
    )(gx, conv_w, row(conv_b), blockdiag(wa), row(ba), blockdiag(wx), row(bx), row(lam),
      qk, vg, cos, sin, row(gn_w), row(gn_b))


def _ffn_kernel(x_ref, xh_ref, oa_ref, oah_ref, ob_ref, obh_ref, wo_ref, g_ref, wup_ref, wc_ref, wd_ref,
                *rest, seq_blocks, final):
    if final:
        gf_ref, y_ref, hc_ref, u_ref = rest
    else:
        y_ref, hc_ref, u_ref = rest
    rows = x_ref.shape[0]
    half = oa_ref.shape[1]
    ffn = wd_ref.shape[0]

    def mixed(x, oa, ob):
        return (x + jnp.dot(oa, wo_ref[0:half, :], preferred_element_type=F32)
                + jnp.dot(ob, wo_ref[half:2 * half, :], preferred_element_type=F32))

    x1 = mixed(x_ref[...], oa_ref[...], ob_ref[...])
    x1h = mixed(xh_ref[...], oah_ref[...], obh_ref[...])
    starts_seq = (pl.program_id(0) % seq_blocks) == 0
    hc_ref[0:BF16_ROWS, :] = (_rms(x1h, g_ref[...]) * jnp.where(starts_seq, 0.0, 1.0)).astype(BF16)
    hc_ref[BF16_ROWS:BF16_ROWS + rows, :] = _rms(x1, g_ref[...]).astype(BF16)

    u_ref[...] = jnp.dot(hc_ref[...], wup_ref[:, 0:ffn], preferred_element_type=F32)
    v = jnp.dot(hc_ref[BF16_ROWS:BF16_ROWS + rows, :], wup_ref[:, ffn:2 * ffn], preferred_element_type=F32)
    uc = wc_ref[FFN_CONV - 1:FFN_CONV, :] * u_ref[pl.ds(BF16_ROWS, rows), :]
    for back in range(1, FFN_CONV):
        uc = uc + wc_ref[FFN_CONV - 1 - back:FFN_CONV - back, :] * u_ref[pl.ds(BF16_ROWS - back, rows), :]
    gl = (_silu(uc) * v).astype(BF16)
    y = x1 + jnp.dot(gl, wd_ref[...], preferred_element_type=F32)
    if final:
        y = _rms(y, gf_ref[...])
    y_ref[...] = y


def _ffn(x, oa, ob, w_out, norm_w, w_up, w_conv, w_down, seq, final_norm=None):
    m, d = x.shape
    half = oa.shape[1]
    ffn = w_down.shape[0]
    tm = FFN_ROWS
    halo = BF16_ROWS
    final = final_norm is not None
    row_spec = lambda w: pl.BlockSpec((tm, w), lambda i: (i, 0))
    halo_spec = lambda w: pl.BlockSpec((halo, w), lambda i: (jnp.maximum(i * (tm // halo) - 1, 0), 0))
    resident = lambda w: pl.BlockSpec(w.shape, lambda i: (0, 0), pipeline_mode=pl.Buffered(1))
    row = lambda p: p.reshape(1, -1).astype(F32)
    in_specs = [row_spec(d), halo_spec(d), row_spec(half), halo_spec(half), row_spec(half), halo_spec(half),
                resident(w_out), _const_spec((1, d)), resident(w_up), _const_spec(w_conv.shape), resident(w_down)]
    args = [x, x, oa, oa, ob, ob, w_out, row(norm_w), w_up, w_conv, w_down]
    if final:
        in_specs.append(_const_spec((1, d)))
        args.append(row(final_norm))
    return pl.pallas_call(
        functools.partial(_ffn_kernel, seq_blocks=seq // tm, final=final),
        grid=(m // tm,),
        in_specs=in_specs,
        out_specs=row_spec(d),
        out_shape=jax.ShapeDtypeStruct((m, d), F32),
        scratch_shapes=[pltpu.VMEM((tm + halo, d), BF16), pltpu.VMEM((tm + halo, ffn), F32)],
        compiler_params=_params(("parallel",)),
        name="outproj_ffn_final" if final else "outproj_ffn",
    )(*args)


def kernel(x, l0_norm1, l0_w_in, l0_a_conv, l0_a_A_log, l0_a_dt_bias, l0_a_norm, l0_b_mu, l0_b_w0, l0_b_w2, l0_b_a0, l0_b_a2, l0_b_g2, l0_b_k_k, l0_b_k_a, l0_b_r_k, l0_b_ln_w, l0_b_ln_b, l0_w_out, l0_norm2, l0_ffn_up, l0_ffn_conv, l0_ffn_down, l1_norm1, l1_w_in, l1_c_conv_w, l1_c_conv_b, l1_c_wa, l1_c_ba, l1_c_wx, l1_c_bx, l1_c_lambda, l1_d_gn_w, l1_d_gn_b, l1_w_out, l1_norm2, l1_ffn_up, l1_ffn_conv, l1_ffn_down, final_norm):
    batch, seq, d = x.shape
    xf = x.reshape(batch * seq, d)
    bf = lambda w: w.astype(BF16)

    a_w = l0_a_conv.shape[1] // 3
    a_heads = l0_a_A_log.shape[0]
    a_in = 4 * a_w + 2 * a_heads
    w_gates = jnp.concatenate([bf(l0_w_in[:, 4 * a_w:a_in]), jnp.zeros((d, LANES - 2 * a_heads), BF16)], axis=1)
    b_in = l0_w_in.shape[1] - a_in
    qkvz, ba, pb = _norm_proj(xf, l0_norm1, [bf(l0_w_in[:, 0:4 * a_w]), w_gates, bf(l0_w_in[:, a_in:])],
                              ((4 * a_w,), (LANES,), (b_in,)), "norm_proj0")
    o_a = _gdn(qkvz, ba, l0_a_conv, l0_a_A_log, l0_a_dt_bias, l0_a_norm, batch)
    o_bb = _rwkv(pb, l0_b_mu, l0_b_w0, bf(l0_b_w2), l0_b_a0, bf(l0_b_a2), bf(l0_b_g2),
                 l0_b_k_k, l0_b_k_a, l0_b_r_k, l0_b_ln_w, l0_b_ln_b, batch)
    x1 = _ffn(xf, o_a, o_bb, bf(l0_w_out), l0_norm2, bf(l0_ffn_up), l0_ffn_conv, bf(l0_ffn_down), seq)

    c_w = l1_c_lambda.shape[0]
    d_w = l1_d_gn_w.shape[0]
    qk_w = l1_w_in.shape[1] - 2 * c_w - 2 * d_w
    gx, qk, vg = _norm_proj(x1, l1_norm1, [bf(l1_w_in)], ((2 * c_w, qk_w, 2 * d_w),), "norm_proj1")
    o_c, o_d = _mixer_cd(gx, qk, vg, l1_c_conv_w, l1_c_conv_b, l1_c_wa, l1_c_ba, l1_c_wx, l1_c_bx, l1_c_lambda,
                         l1_d_gn_w, l1_d_gn_b, batch)
    y = _ffn(x1, o_c, o_d, bf(l1_w_out), l1_norm2, bf(l1_ffn_up), l1_ffn_conv, bf(l1_ffn_down), seq,
             final_norm=final_norm)
    return y.reshape(batch, seq, d)
```

```python
import functools
import math

import numpy as np
import jax
import jax.numpy as jnp
from jax import lax
from jax.experimental import pallas as pl
from jax.experimental.pallas import tpu as pltpu

F32 = jnp.float32
BF16 = jnp.bfloat16

EPS = 1e-6
CHUNK = 64
A_HEAD_DIM = 128
A_CONV = 4
B_HEAD_DIM = 64
B_LN_EPS = 64e-5
C_GATE_SCALE = 8.0
C_CONV = 4
D_KEY_DIM = 64
D_VAL_DIM = 128
D_HEADS = 4
ROPE_BASE = 10000.0
FFN_CONV = 3

LANES = 128
SUBLANES = 8
BF16_ROWS = 16
VMEM_LIMIT = 56 * 1024 * 1024

MIX_ROWS = 256
MIX_SEQS = 2
PROJ_ROWS = 256
FFN_ROWS = 512

NN = (((1,), (0,)), ((), ()))
NT = (((1,), (1,)), ((), ()))
TN = (((0,), (0,)), ((), ()))


def _pieces(x, n):
    if x.dtype == BF16:
        return [x]
    out, r = [], x
    for i in range(n):
        p = r.astype(BF16)
        out.append(p)
        if i + 1 < n:
            r = r - p.astype(F32)
    return out


def _dot(a, b, dims=NN, pa=1, pb=1):
    ap, bp = _pieces(a, pa), _pieces(b, pb)
    order = max(len(ap), len(bp))
    acc = None
    for i in reversed(range(len(ap))):
        for j in reversed(range(len(bp))):
            if i + j < order:
                t = lax.dot_general(ap[i], bp[j], dims, preferred_element_type=F32)
                acc = t if acc is None else acc + t
    return acc


def _sigmoid(x):
    return 1.0 / (1.0 + jnp.exp(-x))


def _silu(x):
    return x * _sigmoid(x)


def _softplus(x):
    return jnp.maximum(x, 0.0) + jnp.log1p(jnp.exp(-jnp.abs(x)))


def _gelu_tanh(x):
    return 0.5 * x * (1.0 + jnp.tanh(math.sqrt(2.0 / math.pi) * (x + 0.044715 * (x * x * x))))


def _rms(x, g):
    return x * lax.rsqrt(jnp.mean(x * x, axis=-1, keepdims=True) + EPS) * g


def _iota(shape, dim):
    return lax.broadcasted_iota(jnp.int32, shape, dim)


def _div(i, n):
    assert n & (n - 1) == 0
    return i >> (n.bit_length() - 1)


def _mod(i, n):
    assert n & (n - 1) == 0
    return i & (n - 1)


def _pair_masks():
    t, s = _iota((CHUNK, 2 * CHUNK), 0), _mod(_iota((CHUNK, 2 * CHUNK), 1), CHUNK)
    return s < t, s <= t


def _stack_heads(x, n):
    head = _div(_iota(x.shape, 1), x.shape[1] // n)
    return jnp.concatenate([jnp.where(head == h, x, 0.0) for h in range(n)], axis=0)


def _stack_pair(x):
    return _stack_heads(x, 2)


def _inv_unit_lower(n_mats):
    shape = n_mats[0].shape
    heads = shape[1] // CHUNK
    stack = functools.partial(_stack_heads, n=heads)
    eye = (_mod(_iota(shape, 1), CHUNK) == _iota(shape, 0)).astype(F32)
    ts = [eye + n for n in n_mats]
    ps = [_dot(n, stack(n)) for n in n_mats]
    levels = int(math.log2(CHUNK)) - 1
    for level in range(levels):
        last = level == levels - 1
        lhs = ts if last else [jnp.concatenate([t, p], axis=0) for t, p in zip(ts, ps)]
        prod = [_dot(l, stack(p)) for l, p in zip(lhs, ps)]
        ts = [t + pr[0:CHUNK] for t, pr in zip(ts, prod)]
        if not last:
            ps = [pr[CHUNK:2 * CHUNK] for pr in prod]
    return ts


def _head_sums(x, dim):
    assert dim in (LANES, LANES // 2)
    low = _iota((x.shape[0], LANES), 1) < dim
    parts = []
    for c in range(x.shape[1] // LANES):
        xc = x[:, c * LANES:(c + 1) * LANES]
        total = jnp.sum(xc, axis=-1, keepdims=True)
        if dim == LANES:
            parts.append(jnp.broadcast_to(total, xc.shape))
        else:
            first = jnp.sum(jnp.where(low, xc, 0.0), axis=-1, keepdims=True)
            parts.append(jnp.where(low, first, total - first))
    return jnp.concatenate(parts, axis=1)


def _head_expand(cols, first, heads, dim):
    return jnp.concatenate([jnp.broadcast_to(cols[:, first + h:first + h + 1], (cols.shape[0], dim))
                            for h in range(heads)], axis=1)


def _shifted_rows(xp_ref, back, rows):
    if back == 0:
        return xp_ref[SUBLANES:SUBLANES + rows, :]
    return pltpu.roll(xp_ref[...], back, 0)[SUBLANES:SUBLANES + rows, :]


def _params(sem):
    return pltpu.CompilerParams(dimension_semantics=sem, vmem_limit_bytes=VMEM_LIMIT)


def _const_spec(shape):
    return pl.BlockSpec(shape, lambda *_: (0,) * len(shape))


def _chunk_sum_matrix(rows):
    t = np.arange(rows)
    same = (t[:, None] // CHUNK) == (t[None, :] // CHUNK)
    return jnp.asarray(np.concatenate([same & (t[None, :] <= t[:, None]), same], axis=0), BF16)


def _norm_proj_kernel(x_ref, g_ref, *refs, splits):
    w_refs, o_refs = refs[:len(splits)], refs[len(splits):]
    h = _rms(x_ref[...], g_ref[...]).astype(BF16)
    outs = iter(o_refs)
    for w_ref, cols in zip(w_refs, splits):
        off = 0
        for n in cols:
            next(outs)[...] = jnp.dot(h, w_ref[:, off:off + n], preferred_element_type=F32)
            off += n


def _norm_proj(x, g, ws, splits, name):
    m, d = x.shape
    tm = PROJ_ROWS
    widths = [n for cols in splits for n in cols]
    return pl.pallas_call(
        functools.partial(_norm_proj_kernel, splits=splits),
        grid=(m // tm,),
        in_specs=[pl.BlockSpec((tm, d), lambda i: (i, 0)), _const_spec((1, d))] + [_const_spec(w.shape) for w in ws],
        out_specs=[pl.BlockSpec((tm, n), lambda i: (i, 0)) for n in widths],
        out_shape=[jax.ShapeDtypeStruct((m, n), F32) for n in widths],
        compiler_params=_params(("parallel",)),
        name=name,
    )(x, g.reshape(1, d), *ws)


def _gdn_kernel(qkvz_ref, ba_ref, cw_ref, alog_ref, dtb_ref, nw_ref, lcat_ref,
                o_ref, xp_ref, s_ref, qs_ref, k_ref, kb_ref, vb_ref, kbg_ref, qd_ref, gc_ref,
                gl_ref, us_ref, wq_ref, qkm_ref, kdt_ref):
    n_seq, rows = qkvz_ref.shape[0], qkvz_ref.shape[1]
    width = o_ref.shape[2]
    n_heads = width // A_HEAD_DIM
    tb = pl.program_id(1)

    @pl.when(tb == 0)
    def _():
        xp_ref[:, 0:SUBLANES, :] = jnp.zeros((n_seq, SUBLANES, xp_ref.shape[2]), F32)
        s_ref[...] = jnp.zeros(s_ref.shape, F32)

    def l2n(x):
        return x * lax.rsqrt(_head_sums(x * x, A_HEAD_DIM) + EPS)

    for q_i in range(n_seq):
        at = slice(q_i * rows, (q_i + 1) * rows)
        xp = xp_ref.at[q_i]
        xp[SUBLANES:SUBLANES + rows, :] = qkvz_ref[q_i, :, 0:3 * width]
        acc = cw_ref[A_CONV - 1:A_CONV, :] * _shifted_rows(xp, 0, rows)
        for back in range(1, A_CONV):
            acc = acc + cw_ref[A_CONV - 1 - back:A_CONV - back, :] * _shifted_rows(xp, back, rows)
        xp[0:SUBLANES, :] = xp[rows:rows + SUBLANES, :]
        act = _silu(acc)
        q, k, v = act[:, 0:width], act[:, width:2 * width], act[:, 2 * width:3 * width]

        ba = ba_ref[q_i]
        bg = jnp.where(_iota(ba.shape, 1) < n_heads, _sigmoid(ba),
                       -jnp.exp(alog_ref[...]) * _softplus(ba + dtb_ref[...]))
        ct = _dot(lcat_ref[...], bg, pb=3)
        beta_f = _head_expand(bg, 0, n_heads, A_HEAD_DIM)
        gc_f = _head_expand(ct[0:rows], n_heads, n_heads, A_HEAD_DIM)
        gl_f = _head_expand(ct[rows:2 * rows], n_heads, n_heads, A_HEAD_DIM)

        kn = l2n(k)
        qs = l2n(q) * (A_HEAD_DIM ** -0.5)
        kb = kn * beta_f
        e_gc = jnp.exp(gc_f)
        qs_ref[at, :] = qs
        k_ref[at, :] = kn
        kb_ref[at, :] = kb
        vb_ref[at, :] = v * beta_f
        kbg_ref[at, :] = kb * e_gc
        qd_ref[at, :] = qs * e_gc
        gc_ref[at, :] = gc_f
        gl_ref[at, :] = gl_f

    pair_w = 2 * A_HEAD_DIM
    assert A_HEAD_DIM == 2 * CHUNK
    n_pairs = width // pair_w
    n_chunks = rows // CHUNK
    strict, lower = _pair_masks()
    first_head = _iota((CHUNK, 2 * CHUNK), 1) < CHUNK
    same_head = (_iota((pair_w, pair_w), 0) < A_HEAD_DIM) == (_iota((pair_w, pair_w), 1) < A_HEAD_DIM)

    def ld(ref, c, p):
        return ref[c * CHUNK:(c + 1) * CHUNK, p * pair_w:(p + 1) * pair_w]

    keys = [(c, p) for c in range(n_seq * n_chunks) for p in range(n_pairs)]
    g = [_dot(jnp.concatenate([ld(kb_ref, *k), ld(qs_ref, *k)], axis=0), _stack_pair(ld(k_ref, *k)), NT)
         for k in keys]
    n_ms = []
    for i, k in enumerate(keys):
        gcp = ld(gc_ref, *k)
        g_t = jnp.where(first_head, gcp[:, 0:A_HEAD_DIM], gcp[:, A_HEAD_DIM:pair_w])
        g_s = jnp.concatenate([gcp[:, 0:A_HEAD_DIM], gcp[:, A_HEAD_DIM:pair_w]], axis=0).T[0:CHUNK]
        dec = jnp.exp(jnp.where(lower, g_t - g_s, 0.0))
        n_ms.append(jnp.where(strict, -(g[i][0:CHUNK] * dec), 0.0))
        qkm_ref[i] = jnp.where(lower, g[i][CHUNK:2 * CHUNK] * dec, 0.0).astype(BF16)
    t_ms = _inv_unit_lower(n_ms)
    for i, k in enumerate(keys):
        uw = _dot(t_ms[i], jnp.concatenate([_stack_pair(ld(vb_ref, *k)), _stack_pair(ld(kbg_ref, *k))], axis=1))
        us_ref[i] = uw[:, 0:pair_w]
        wq_ref[i] = jnp.concatenate([uw[:, pair_w:2 * pair_w], ld(qd_ref, *k)], axis=0).astype(BF16)
        kdt_ref[i] = (ld(k_ref, *k) * jnp.exp(ld(gl_ref, *k) - ld(gc_ref, *k))).T.astype(BF16)

    chains = [(q_i, p) for q_i in range(n_seq) for p in range(n_pairs)]
    s = [s_ref[q_i * n_pairs + p] for q_i, p in chains]
    for c in range(n_chunks):
        idx = [(q_i * n_chunks + c) * n_pairs + p for q_i, p in chains]
        x = [_dot(wq_ref[i], s[n]) for n, i in enumerate(idx)]
        v_new = [us_ref[i] - x[n][0:CHUNK] for n, i in enumerate(idx)]
        o_c = [x[n][CHUNK:2 * CHUNK] + _dot(qkm_ref[i], _stack_pair(v_new[n])) for n, i in enumerate(idx)]
        for n, ((q_i, p), i) in enumerate(zip(chains, idx)):
            r0 = (q_i * n_chunks + c) * CHUNK
            g_last = jnp.exp(gl_ref[r0:r0 + 1, p * pair_w:(p + 1) * pair_w])
            s[n] = s[n] * g_last + jnp.where(same_head, _dot(kdt_ref[i], v_new[n]), 0.0)
        for q_i in range(n_seq):
            o = jnp.concatenate(o_c[q_i * n_pairs:(q_i + 1) * n_pairs], axis=1)
            ms = _head_sums(o * o, A_HEAD_DIM) * (1.0 / A_HEAD_DIM)
            z = qkvz_ref[q_i, c * CHUNK:(c + 1) * CHUNK, 3 * width:4 * width]
            o_ref[q_i, c * CHUNK:(c + 1) * CHUNK, :] = (
                o * lax.rsqrt(ms + EPS) * nw_ref[...] * _silu(z)).astype(o_ref.dtype)
    for n in range(len(chains)):
        s_ref[n] = s[n]


def _seq_groups(x, n_seq):
    return x.reshape(n_seq, x.shape[0] // n_seq, x.shape[1])


def _gdn(qkvz, ba, conv_w, a_log, dt_bias, norm_w, batch):
    m = qkvz.shape[0]
    width = qkvz.shape[1] // 4
    n_heads = width // A_HEAD_DIM
    rows = MIX_ROWS
    n_seq = MIX_SEQS
    tblocks = m // batch // rows
    pad = lambda p: jnp.zeros((1, LANES), F32).at[0, n_heads:2 * n_heads].set(p)
    lcat = _chunk_sum_matrix(rows)
    row_spec = lambda w: pl.BlockSpec((n_seq, rows, w), lambda b, i: (0, b * tblocks + i, 0))
    act = lambda: pltpu.VMEM((n_seq * rows, width), F32)
    pair_w = 2 * A_HEAD_DIM
    n_pairs = width // pair_w
    n_keys = n_seq * (rows // CHUNK) * n_pairs
    out = pl.pallas_call(
        _gdn_kernel,
        grid=(batch // n_seq, tblocks),
        in_specs=[row_spec(4 * width), row_spec(LANES), _const_spec(conv_w.shape), _const_spec((1, LANES)),
                  _const_spec((1, LANES)), _const_spec((1, width)), _const_spec(lcat.shape)],
        out_specs=row_spec(width),
        out_shape=jax.ShapeDtypeStruct((n_seq, m // n_seq, width), BF16),
        scratch_shapes=[pltpu.VMEM((n_seq, rows + SUBLANES, 3 * width), F32),
                        pltpu.VMEM((n_seq * n_pairs, pair_w, pair_w), F32)]
                       + [act() for _ in range(8)]
                       + [pltpu.VMEM((n_keys, CHUNK, pair_w), F32), pltpu.VMEM((n_keys, 2 * CHUNK, pair_w), BF16),
                          pltpu.VMEM((n_keys, CHUNK, 2 * CHUNK), BF16), pltpu.VMEM((n_keys, pair_w, CHUNK), BF16)],
        compiler_params=_params(("arbitrary", "arbitrary")),
        name="gdn_mixer",
    )(_seq_groups(qkvz, n_seq), _seq_groups(ba, n_seq), conv_w, pad(a_log), pad(dt_bias),
      jnp.tile(norm_w, n_heads).reshape(1, width), lcat)
    return out.reshape(m, width)


def _rwkv_kernel(pb_ref, mu_ref, w0_ref, w2_ref, a0_ref, a2_ref, g2_ref, kk_ref, ka_ref, rk_ref, lnw_ref,
                 lnb_ref, lcat_ref, o_ref, xp_ref, s_ref, at_ref, rt_ref, bt_ref, kt_ref,
                 v_ref, bv_ref, km_ref, cum_ref, tot_ref, r_ref, gate_ref, lx_ref, tw_ref, mc_ref, bkt_ref,
                 pcol_ref):
    n_seq, rows = pb_ref.shape[0], pb_ref.shape[1]
    width = o_ref.shape[2]
    tb = pl.program_id(1)

    @pl.when(tb == 0)
    def _():
        xp_ref[:, 0:SUBLANES, :] = jnp.zeros((n_seq, SUBLANES, xp_ref.shape[2]), F32)
        s_ref[...] = jnp.zeros(s_ref.shape, F32)

    def gsum(x):
        return _head_sums(x, B_HEAD_DIM)

    for q_i in range(n_seq):
        at = slice(q_i * rows, (q_i + 1) * rows)
        xp = xp_ref.at[q_i]
        p = pb_ref[q_i]
        xp[SUBLANES:SUBLANES + rows, :] = p
        prev = _shifted_rows(xp, 1, rows)
        xp[0:SUBLANES, :] = xp[rows:rows + SUBLANES, :]
        m = p + (prev - p) * mu_ref[...]
        r, kr, vr = m[:, 0:width], m[:, width:2 * width], m[:, 2 * width:3 * width]
        o1 = 3 * width
        o2 = o1 + w2_ref.shape[0]
        o3 = o2 + a2_ref.shape[0]
        w_lo, a_lo, g_lo = m[:, o1:o2], m[:, o2:o3], m[:, o3:]

        lw = -math.exp(-0.5) * _sigmoid(w0_ref[...] + _dot(jnp.tanh(w_lo), w2_ref[...]))
        a_lr = _sigmoid(a0_ref[...] + _dot(a_lo, a2_ref[...]))
        gate_ref[at, :] = _dot(_sigmoid(g_lo), g2_ref[...])

        kk = kr * kk_ref[...]
        k_mod = kr * (1.0 + (a_lr - 1.0) * ka_ref[...])
        kk = kk * lax.rsqrt(gsum(kk * kk) + EPS)
        a_vec = -kk
        b_vec = kk * a_lr

        ct = _dot(lcat_ref[...], lw, pb=2)
        cum, tot = ct[0:rows], ct[rows:2 * rows]
        e_neg = jnp.exp(-cum)
        rt_ref[at, :] = r * jnp.exp(cum)
        at_ref[at, :] = a_vec * jnp.exp(cum - lw)
        bt_ref[at, :] = b_vec * e_neg
        kt_ref[at, :] = k_mod * e_neg
        bv_ref[at, :] = b_vec
        km_ref[at, :] = k_mod
        cum_ref[at, :] = cum
        tot_ref[at, :] = tot
        r_ref[at, :] = r
        v_ref[at, :] = vr

    pair_w = 2 * B_HEAD_DIM
    n_pairs = width // pair_w
    n_chunks = rows // CHUNK
    strict, lower = _pair_masks()
    lower2 = jnp.concatenate([lower, lower], axis=1)
    same_head = (_iota((pair_w, pair_w), 0) < B_HEAD_DIM) == (_iota((pair_w, pair_w), 1) < B_HEAD_DIM)

    def ld(ref, c, pr):
        return ref[c * CHUNK:(c + 1) * CHUNK, pr * pair_w:(pr + 1) * pair_w]

    for q_i in range(n_seq):
        keys = [(q_i * n_chunks + c, pr) for c in range(n_chunks) for pr in range(n_pairs)]
        g = [_dot(jnp.concatenate([ld(at_ref, *k), ld(rt_ref, *k)], axis=0),
                  jnp.concatenate([_stack_pair(ld(bt_ref, *k)), _stack_pair(ld(kt_ref, *k))], axis=0), NT)
             for k in keys]
        t_ms = _inv_unit_lower([jnp.where(strict, gi[0:CHUNK, 0:2 * CHUNK], 0.0) for gi in g])
        w1 = [_dot(jnp.where(strict, g[i][0:CHUNK, 2 * CHUNK:4 * CHUNK], 0.0), _stack_pair(ld(v_ref, *k)))
              for i, k in enumerate(keys)]
        taw = [_dot(t_ms[i], jnp.concatenate([_stack_pair(ld(at_ref, *k)), _stack_pair(w1[i])], axis=1))
               for i, k in enumerate(keys)]
        for i, k in enumerate(keys):
            j = k[0] * n_pairs + k[1]
            lx_ref[j] = jnp.concatenate([taw[i][:, 0:pair_w], ld(rt_ref, *k)], axis=0).astype(BF16)
            tw_ref[j] = taw[i][:, pair_w:2 * pair_w]
            mc_ref[j] = jnp.where(lower2, g[i][CHUNK:2 * CHUNK], 0.0).astype(BF16)
            tot_c = ld(tot_ref, *k)
            e_dec = jnp.exp(tot_c - ld(cum_ref, *k))
            bkt_ref[j] = jnp.concatenate([ld(bv_ref, *k) * e_dec, ld(km_ref, *k) * e_dec], axis=0).T.astype(BF16)
            pcol_ref[j] = jnp.broadcast_to(jnp.exp(tot_c[0:SUBLANES].T[:, 0:1]), (pair_w, pair_w))

    chains = [(q_i, pr) for q_i in range(n_seq) for pr in range(n_pairs)]
    s = [s_ref[q_i * n_pairs + pr] for q_i, pr in chains]
    for c in range(n_chunks):
        gc = [q_i * n_chunks + c for q_i, _ in chains]
        idx = [gc[n] * n_pairs + pr for n, (_, pr) in enumerate(chains)]
        v_c = [ld(v_ref, gc[n], pr) for n, (_, pr) in enumerate(chains)]
        x = [_dot(lx_ref[j], s[n]) for n, j in enumerate(idx)]
        u = [x[n][0:CHUNK] + tw_ref[j] for n, j in enumerate(idx)]
        y_c = [x[n][CHUNK:2 * CHUNK]
               + _dot(mc_ref[j], jnp.concatenate([_stack_pair(u[n]), _stack_pair(v_c[n])], axis=0))
               for n, j in enumerate(idx)]
        for n, j in enumerate(idx):
            upd = _dot(bkt_ref[j], jnp.concatenate([u[n], v_c[n]], axis=0))
            s[n] = s[n] * pcol_ref[j] + jnp.where(same_head, upd, 0.0)
        for q_i in range(n_seq):
            rows_c = slice((q_i * n_chunks + c) * CHUNK, (q_i * n_chunks + c + 1) * CHUNK)
            y = jnp.concatenate(y_c[q_i * n_pairs:(q_i + 1) * n_pairs], axis=1)
            inv_n = 1.0 / B_HEAD_DIM
            mu = gsum(y) * inv_n
            yc = y - mu
            var = gsum(yc * yc) * inv_n
            yn = yc * lax.rsqrt(var + B_LN_EPS) * lnw_ref[...] + lnb_ref[...]
            bonus = gsum(r_ref[rows_c, :] * km_ref[rows_c, :] * rk_ref[...]) * v_ref[rows_c, :]
            o_ref[q_i, c * CHUNK:(c + 1) * CHUNK, :] = ((yn + bonus) * gate_ref[rows_c, :]).astype(o_ref.dtype)
    for n in range(len(chains)):
        s_ref[n] = s[n]


def _rwkv(pb, mu, w0, w2, a0, a2, g2, k_k, k_a, r_k, ln_w, ln_b, batch):
    m, in_w = pb.shape
    width = w0.shape[0]
    rows = MIX_ROWS
    n_seq = MIX_SEQS
    tblocks = m // batch // rows
    lcat = _chunk_sum_matrix(rows)
    row = lambda p: p.reshape(1, -1).astype(F32)
    row_spec = lambda w: pl.BlockSpec((n_seq, rows, w), lambda b, i: (0, b * tblocks + i, 0))
    vec = _const_spec((1, width))
    act = lambda: pltpu.VMEM((n_seq * rows, width), F32)
    pair_w = 2 * B_HEAD_DIM
    n_pairs = width // pair_w
    n_keys = n_seq * (rows // CHUNK) * n_pairs
    out = pl.pallas_call(
        _rwkv_kernel,
        grid=(batch // n_seq, tblocks),
        in_specs=[row_spec(in_w), _const_spec((1, in_w)), vec, _const_spec(w2.shape), vec,
                  _const_spec(a2.shape), _const_spec(g2.shape), vec, vec, vec, vec, vec,
                  _const_spec(lcat.shape)],
        out_specs=row_spec(width),
        out_shape=jax.ShapeDtypeStruct((n_seq, m // n_seq, width), BF16),
        scratch_shapes=[pltpu.VMEM((n_seq, rows + SUBLANES, in_w), F32),
                        pltpu.VMEM((n_seq * n_pairs, pair_w, pair_w), F32)]
                       + [act() for _ in range(11)]
                       + [pltpu.VMEM((n_keys, 2 * CHUNK, pair_w), BF16), pltpu.VMEM((n_keys, CHUNK, pair_w), F32),
                          pltpu.VMEM((n_keys, CHUNK, 4 * CHUNK), BF16), pltpu.VMEM((n_keys, pair_w, 2 * CHUNK), BF16),
                          pltpu.VMEM((n_keys, pair_w, pair_w), F32)],
        compiler_params=_params(("arbitrary", "arbitrary")),
        name="rwkv_mixer",
    )(_seq_groups(pb, n_seq), row(mu), row(w0), w2, row(a0), a2, g2, row(k_k), row(k_a), row(r_k), row(ln_w),
      row(ln_b), lcat)
    return out.reshape(m, width)


def _rglru_body(gx_ref, cw_ref, cb_ref, wa_ref, ba_ref, wx_ref, bx_ref, lam_ref, o_ref, xp_ref, h_ref):
    rows = gx_ref.shape[0]
    width = o_ref.shape[1]
    xp_ref[SUBLANES:SUBLANES + rows, :] = gx_ref[:, width:2 * width]
    xc = cb_ref[...] + cw_ref[C_CONV - 1:C_CONV, :] * _shifted_rows(xp_ref, 0, rows)
    for back in range(1, C_CONV):
        xc = xc + cw_ref[C_CONV - 1 - back:C_CONV - back, :] * _shifted_rows(xp_ref, back, rows)
    xp_ref[0:SUBLANES, :] = xp_ref[rows:rows + SUBLANES, :]

    gate_r = _sigmoid(_dot(xc, wa_ref[...]) + ba_ref[...])
    gate_i = _sigmoid(_dot(xc, wx_ref[...]) + bx_ref[...])
    log_a = -C_GATE_SCALE * gate_r * _softplus(-lam_ref[...])
    a = jnp.exp(log_a)
    d = jnp.sqrt(jnp.tanh(-log_a) * (a * a + 1.0)) * (gate_i * xc)

    t_idx = _iota((rows, width), 0)
    shift = 1
    while shift < SUBLANES:
        keep = t_idx >= shift
        d = jnp.where(keep, a * pltpu.roll(d, shift, 0) + d, d)
        a = jnp.where(keep, a * pltpu.roll(a, shift, 0), a)
        shift *= 2
    while shift < rows:
        d = jnp.concatenate([d[:shift], a[shift:] * d[:rows - shift] + d[shift:]], axis=0)
        a = jnp.concatenate([a[:shift], a[shift:] * a[:rows - shift]], axis=0)
        shift *= 2
    h = a * h_ref[0:1, :] + d
    h_ref[...] = jnp.broadcast_to(h[rows - 1:rows, :], h_ref.shape)
    o_ref[...] = (h * _gelu_tanh(gx_ref[:, 0:width])).astype(o_ref.dtype)


_LOG_GAMMA = tuple(math.log(1.0 - 2.0 ** (-5.0 - h)) for h in range(D_HEADS))


def _per_head(lane_head, values):
    out = jnp.full(lane_head.shape, values[-1], F32)
    for h in range(len(values) - 2, -1, -1):
        out = jnp.where(lane_head == h, values[h], out)
    return out


def _retention_body(qk_ref, vg_ref, cos_ref, sin_ref, gw_ref, gb_ref, o_ref, s_ref, dm_ref):
    rows = qk_ref.shape[0]
    kw = qk_ref.shape[1] // 2
    vw = o_ref.shape[1]

    def rotary(x):
        even = _mod(_iota((rows, LANES), 1), 2) == 0
        parts = []
        for c in range(kw // LANES):
            xc = x[:, c * LANES:(c + 1) * LANES]
            parts.append(jnp.where(even, pltpu.roll(xc, LANES - 1, 1), pltpu.roll(xc, 1, 1)))
        return x * cos_ref[...] + jnp.concatenate(parts, axis=1) * sin_ref[...]

    q = rotary(qk_ref[:, 0:kw])
    k = rotary(qk_ref[:, kw:2 * kw]) * (D_KEY_DIM ** -0.5)
    v = vg_ref[:, 0:vw]
    k_head = _div(_iota((1, kw), 1), D_KEY_DIM)
    lg_k = _per_head(k_head, _LOG_GAMMA)
    lg_v = _per_head(_div(_iota((1, vw), 1), D_VAL_DIM), _LOG_GAMMA)
    t_idx = _iota((rows, 1), 0).astype(F32)

    o_cross = _dot(q * jnp.exp((t_idx + 1.0) * lg_k), s_ref[...])
    inner = []
    for h in range(D_HEADS):
        scores = _dot(jnp.where(k_head == h, q, 0.0), k, NT) * dm_ref[h]
        inner.append(_dot(scores, v[:, h * D_VAL_DIM:(h + 1) * D_VAL_DIM]))
    o = o_cross + jnp.concatenate(inner, axis=1)

    k_dec = k * jnp.exp((rows - 1.0 - t_idx) * lg_k)
    same_head = _div(_iota((kw, vw), 0), D_KEY_DIM) == _div(_iota((kw, vw), 1), D_VAL_DIM)
    s_ref[...] = s_ref[...] * jnp.exp(float(rows) * lg_v) + jnp.where(same_head, _dot(k_dec, v, TN), 0.0)

    inv_n = 1.0 / D_VAL_DIM
    mu = _head_sums(o, D_VAL_DIM) * inv_n
    oc = o - mu
    var = _head_sums(oc * oc, D_VAL_DIM) * inv_n
    on = oc * lax.rsqrt(var + EPS) * gw_ref[...] + gb_ref[...]
    o_ref[...] = (_silu(vg_ref[:, vw:2 * vw]) * on).astype(o_ref.dtype)


def _mixer_cd_kernel(gx_ref, cw_ref, cb_ref, wa_ref, ba_ref, wx_ref, bx_ref, lam_ref, qk_ref, vg_ref, cos_ref,
                     sin_ref, gw_ref, gb_ref, oc_ref, od_ref, xp_ref, h_ref, s_ref, dm_ref):
    rows = qk_ref.shape[0]

    @pl.when((pl.program_id(0) == 0) & (pl.program_id(1) == 0))
    def _():
        rel = (_iota((rows, rows), 0) - _iota((rows, rows), 1)).astype(F32)
        for h in range(D_HEADS):
            dm_ref[h] = jnp.where(rel >= 0, jnp.exp(jnp.maximum(rel, 0.0) * _LOG_GAMMA[h]), 0.0)

    @pl.when(pl.program_id(1) == 0)
    def _():
        xp_ref[0:SUBLANES, :] = jnp.zeros((SUBLANES, xp_ref.shape[1]), F32)
        h_ref[...] = jnp.zeros(h_ref.shape, F32)
        s_ref[...] = jnp.zeros(s_ref.shape, F32)

    _retention_body(qk_ref, vg_ref, cos_ref, sin_ref, gw_ref, gb_ref, od_ref, s_ref, dm_ref)
    _rglru_body(gx_ref, cw_ref, cb_ref, wa_ref, ba_ref, wx_ref, bx_ref, lam_ref, oc_ref, xp_ref, h_ref)


def _mixer_cd(gx, qk, vg, conv_w, conv_b, wa, ba, wx, bx, lam, gn_w, gn_b, batch):
    m = qk.shape[0]
    cw = gx.shape[1] // 2
    kw = qk.shape[1] // 2
    vw = vg.shape[1] // 2
    rows = MIX_ROWS
    seq = m // batch
    tblocks = seq // rows
    inv = 1.0 / (ROPE_BASE ** jnp.linspace(0.0, 1.0, D_KEY_DIM // 2, dtype=F32))
    ang = jnp.arange(seq).astype(F32)[:, None] * inv[None, :]
    cos = jnp.tile(jnp.repeat(jnp.cos(ang), 2, axis=1), (1, kw // D_KEY_DIM))
    sin = jnp.tile(jnp.stack([-jnp.sin(ang), jnp.sin(ang)], axis=-1).reshape(seq, D_KEY_DIM),
                   (1, kw // D_KEY_DIM))
    row = lambda p: p.reshape(1, -1).astype(F32)
    blockdiag = lambda w: jax.scipy.linalg.block_diag(*[w[i] for i in range(w.shape[0])]).astype(BF16)
    row_spec = lambda w: pl.BlockSpec((rows, w), lambda b, i: (b * tblocks + i, 0))
    tab_spec = pl.BlockSpec((rows, kw), lambda b, i: (i, 0))
    cvec, cmat, vvec = _const_spec((1, cw)), _const_spec((cw, cw)), _const_spec((1, vw))
    return pl.pallas_call(
        _mixer_cd_kernel,
        grid=(batch, tblocks),
        in_specs=[row_spec(2 * cw), _const_spec(conv_w.shape), cvec, cmat, cvec, cmat, cvec, cvec,
                  row_spec(2 * kw), row_spec(2 * vw), tab_spec, tab_spec, vvec, vvec],
        out_specs=[row_spec(cw), row_spec(vw)],
        out_shape=[jax.ShapeDtypeStruct((m, cw), BF16), jax.ShapeDtypeStruct((m, vw), BF16)],
        scratch_shapes=[pltpu.VMEM((rows + SUBLANES, cw), F32), pltpu.VMEM((SUBLANES, cw), F32),
                        pltpu.VMEM((kw, vw), F32), pltpu.VMEM((D_HEADS, rows, rows), F32)],
        compiler_params=_params(("arbitrary", "arbitrary")),
        name="rglru_retention_mixer",
    )(gx, conv_w, row(conv_b), blockdiag(wa), row(ba), blockdiag(wx), row(bx), row(lam),
      qk, vg, cos, sin, row(gn_w), row(gn_b))


def _ffn_kernel(x_ref, xh_ref, oa_ref, oah_ref, ob_ref, obh_ref, wo_ref, g_ref, wup_ref, wc_ref, wd_ref,
                *rest, seq_blocks, final):
    if final:
        gf_ref, y_ref, hc_ref, u_ref = rest
    else:
        y_ref, hc_ref, u_ref = rest
    rows = x_ref.shape[0]
    half = oa_ref.shape[1]
    ffn = wd_ref.shape[0]

    def mixed(x, oa, ob):
        return (x + jnp.dot(oa, wo_ref[0:half, :], preferred_element_type=F32)
                + jnp.dot(ob, wo_ref[half:2 * half, :], preferred_element_type=F32))

    x1 = mixed(x_ref[...], oa_ref[...], ob_ref[...])
    x1h = mixed(xh_ref[...], oah_ref[...], obh_ref[...])
    starts_seq = (pl.program_id(0) % seq_blocks) == 0
    hc_ref[0:BF16_ROWS, :] = (_rms(x1h, g_ref[...]) * jnp.where(starts_seq, 0.0, 1.0)).astype(BF16)
    hc_ref[BF16_ROWS:BF16_ROWS + rows, :] = _rms(x1, g_ref[...]).astype(BF16)

    u_ref[...] = jnp.dot(hc_ref[...], wup_ref[:, 0:ffn], preferred_element_type=F32)
    v = jnp.dot(hc_ref[BF16_ROWS:BF16_ROWS + rows, :], wup_ref[:, ffn:2 * ffn], preferred_element_type=F32)
    uc = wc_ref[FFN_CONV - 1:FFN_CONV, :] * u_ref[pl.ds(BF16_ROWS, rows), :]
    for back in range(1, FFN_CONV):
        uc = uc + wc_ref[FFN_CONV - 1 - back:FFN_CONV - back, :] * u_ref[pl.ds(BF16_ROWS - back, rows), :]
    gl = (_silu(uc) * v).astype(BF16)
    y = x1 + jnp.dot(gl, wd_ref[...], preferred_element_type=F32)
    if final:
        y = _rms(y, gf_ref[...])
    y_ref[...] = y


def _ffn(x, oa, ob, w_out, norm_w, w_up, w_conv, w_down, seq, final_norm=None):
    m, d = x.shape
    half = oa.shape[1]
    ffn = w_down.shape[0]
    tm = FFN_ROWS
    halo = BF16_ROWS
    final = final_norm is not None
    row_spec = lambda w: pl.BlockSpec((tm, w), lambda i: (i, 0))
    halo_spec = lambda w: pl.BlockSpec((halo, w), lambda i: (jnp.maximum(i * (tm // halo) - 1, 0), 0))
    resident = lambda w: pl.BlockSpec(w.shape, lambda i: (0, 0), pipeline_mode=pl.Buffered(1))
    row = lambda p: p.reshape(1, -1).astype(F32)
    in_specs = [row_spec(d), halo_spec(d), row_spec(half), halo_spec(half), row_spec(half), halo_spec(half),
                resident(w_out), _const_spec((1, d)), resident(w_up), _const_spec(w_conv.shape), resident(w_down)]
    args = [x, x, oa, oa, ob, ob, w_out, row(norm_w), w_up, w_conv, w_down]
    if final:
        in_specs.append(_const_spec((1, d)))
        args.append(row(final_norm))
    return pl.pallas_call(
        functools.partial(_ffn_kernel, seq_blocks=seq // tm, final=final),
        grid=(m // tm,),
        in_specs=in_specs,
        out_specs=row_spec(d),
        out_shape=jax.ShapeDtypeStruct((m, d), F32),
        scratch_shapes=[pltpu.VMEM((tm + halo, d), BF16), pltpu.VMEM((tm + halo, ffn), F32)],
        compiler_params=_params(("parallel",)),
        name="outproj_ffn_final" if final else "outproj_ffn",
    )(*args)


def kernel(x, l0_norm1, l0_w_in, l0_a_conv, l0_a_A_log, l0_a_dt_bias, l0_a_norm, l0_b_mu, l0_b_w0, l0_b_w2, l0_b_a0, l0_b_a2, l0_b_g2, l0_b_k_k, l0_b_k_a, l0_b_r_k, l0_b_ln_w, l0_b_ln_b, l0_w_out, l0_norm2, l0_ffn_up, l0_ffn_conv, l0_ffn_down, l1_norm1, l1_w_in, l1_c_conv_w, l1_c_conv_b, l1_c_wa, l1_c_ba, l1_c_wx, l1_c_bx, l1_c_lambda, l1_d_gn_w, l1_d_gn_b, l1_w_out, l1_norm2, l1_ffn_up, l1_ffn_conv, l1_ffn_down, final_norm):
    batch, seq, d = x.shape
    xf = x.reshape(batch * seq, d)
    bf = lambda w: w.astype(BF16)

    a_w = l0_a_conv.shape[1] // 3
    a_heads = l0_a_A_log.shape[0]
    a_in = 4 * a_w + 2 * a_heads
    w_gates = jnp.concatenate([bf(l0_w_in[:, 4 * a_w:a_in]), jnp.zeros((d, LANES - 2 * a_heads), BF16)], axis=1)
    b_in = l0_w_in.shape[1] - a_in
    qkvz, ba, pb = _norm_proj(xf, l0_norm1, [bf(l0_w_in[:, 0:4 * a_w]), w_gates, bf(l0_w_in[:, a_in:])],
                              ((4 * a_w,), (LANES,), (b_in,)), "norm_proj0")
    o_a = _gdn(qkvz, ba, l0_a_conv, l0_a_A_log, l0_a_dt_bias, l0_a_norm, batch)
    o_bb = _rwkv(pb, l0_b_mu, l0_b_w0, bf(l0_b_w2), l0_b_a0, bf(l0_b_a2), bf(l0_b_g2),
                 l0_b_k_k, l0_b_k_a, l0_b_r_k, l0_b_ln_w, l0_b_ln_b, batch)
    x1 = _ffn(xf, o_a, o_bb, bf(l0_w_out), l0_norm2, bf(l0_ffn_up), l0_ffn_conv, bf(l0_ffn_down), seq)

    c_w = l1_c_lambda.shape[0]
    d_w = l1_d_gn_w.shape[0]
    qk_w = l1_w_in.shape[1] - 2 * c_w - 2 * d_w
    gx, qk, vg = _norm_proj(x1, l1_norm1, [bf(l1_w_in)], ((2 * c_w, qk_w, 2 * d_w),), "norm_proj1")
    o_c, o_d = _mixer_cd(gx, qk, vg, l1_c_conv_w, l1_c_conv_b, l1_c_wa, l1_c_ba, l1_c_wx, l1_c_bx, l1_c_lambda,
                         l1_d_gn_w, l1_d_gn_b, batch)
    y = _ffn(x1, o_c, o_d, bf(l1_w_out), l1_norm2, bf(l1_ffn_up), l1_ffn_conv, bf(l1_ffn_down), seq,
             final_norm=final_norm)
    return y.reshape(batch, seq, d)
```

```python
import functools
import math

import numpy as np
import jax
import jax.numpy as jnp
from jax import lax
from jax.experimental import pallas as pl
from jax.experimental.pallas import tpu as pltpu

F32 = jnp.float32
BF16 = jnp.bfloat16

EPS = 1e-6
CHUNK = 64
A_HEAD_DIM = 128
A_CONV = 4
B_HEAD_DIM = 64
B_LN_EPS = 64e-5
C_GATE_SCALE = 8.0
C_CONV = 4
D_KEY_DIM = 64
D_VAL_DIM = 128
D_HEADS = 4
ROPE_BASE = 10000.0
FFN_CONV = 3

LANES = 128
SUBLANES = 8
BF16_ROWS = 16
VMEM_LIMIT = 56 * 1024 * 1024

MIX_ROWS = 256
MIX_SEQS = 2
GDN_ROWS = 128
GDN_SEQS = 4
PROJ_ROWS = 512
FFN_ROWS = 512

NN = (((1,), (0,)), ((), ()))
NT = (((1,), (1,)), ((), ()))
TN = (((0,), (0,)), ((), ()))


def _pieces(x, n):
    if x.dtype == BF16:
        return [x]
    out, r = [], x
    for i in range(n):
        p = r.astype(BF16)
        out.append(p)
        if i + 1 < n:
            r = r - p.astype(F32)
    return out


def _dot(a, b, dims=NN, pa=1, pb=1):
    ap, bp = _pieces(a, pa), _pieces(b, pb)
    order = max(len(ap), len(bp))
    acc = None
    for i in reversed(range(len(ap))):
        for j in reversed(range(len(bp))):
            if i + j < order:
                t = lax.dot_general(ap[i], bp[j], dims, preferred_element_type=F32)
                acc = t if acc is None else acc + t
    return acc


def _sigmoid(x):
    return 1.0 / (1.0 + jnp.exp(-x))


def _silu(x):
    return x * _sigmoid(x)


def _softplus(x):
    return jnp.maximum(x, 0.0) + jnp.log1p(jnp.exp(-jnp.abs(x)))


def _gelu_tanh(x):
    return 0.5 * x * (1.0 + jnp.tanh(math.sqrt(2.0 / math.pi) * (x + 0.044715 * (x * x * x))))


def _rms(x, g):
    return x * lax.rsqrt(jnp.mean(x * x, axis=-1, keepdims=True) + EPS) * g


def _iota(shape, dim):
    return lax.broadcasted_iota(jnp.int32, shape, dim)


def _div(i, n):
    assert n & (n - 1) == 0
    return i >> (n.bit_length() - 1)


def _mod(i, n):
    assert n & (n - 1) == 0
    return i & (n - 1)


def _pair_masks():
    t, s = _iota((CHUNK, 2 * CHUNK), 0), _mod(_iota((CHUNK, 2 * CHUNK), 1), CHUNK)
    return s < t, s <= t


def _stack_heads(x, n):
    head = _div(_iota(x.shape, 1), x.shape[1] // n)
    return jnp.concatenate([jnp.where(head == h, x, 0.0) for h in range(n)], axis=0)


def _stack_pair(x):
    return _stack_heads(x, 2)


def _inv_unit_lower(n_mats):
    shape = n_mats[0].shape
    heads = shape[1] // CHUNK
    stack = functools.partial(_stack_heads, n=heads)
    eye = (_mod(_iota(shape, 1), CHUNK) == _iota(shape, 0)).astype(F32)
    ts = [eye + n for n in n_mats]
    ps = [_dot(n, stack(n)) for n in n_mats]
    levels = int(math.log2(CHUNK)) - 1
    for level in range(levels):
        last = level == levels - 1
        lhs = ts if last else [jnp.concatenate([t, p], axis=0) for t, p in zip(ts, ps)]
        prod = [_dot(l, stack(p)) for l, p in zip(lhs, ps)]
        ts = [t + pr[0:CHUNK] for t, pr in zip(ts, prod)]
        if not last:
            ps = [pr[CHUNK:2 * CHUNK] for pr in prod]
    return ts


def _head_sums(x, dim):
    assert dim in (LANES, LANES // 2)
    low = _iota((x.shape[0], LANES), 1) < dim
    parts = []
    for c in range(x.shape[1] // LANES):
        xc = x[:, c * LANES:(c + 1) * LANES]
        total = jnp.sum(xc, axis=-1, keepdims=True)
        if dim == LANES:
            parts.append(jnp.broadcast_to(total, xc.shape))
        else:
            first = jnp.sum(jnp.where(low, xc, 0.0), axis=-1, keepdims=True)
            parts.append(jnp.where(low, first, total - first))
    return jnp.concatenate(parts, axis=1)


def _head_expand(cols, first, heads, dim):
    return jnp.concatenate([jnp.broadcast_to(cols[:, first + h:first + h + 1], (cols.shape[0], dim))
                            for h in range(heads)], axis=1)


def _shifted_rows(xp_ref, back, rows):
    if back == 0:
        return xp_ref[SUBLANES:SUBLANES + rows, :]
    return pltpu.roll(xp_ref[...], back, 0)[SUBLANES:SUBLANES + rows, :]


def _params(sem):
    return pltpu.CompilerParams(dimension_semantics=sem, vmem_limit_bytes=VMEM_LIMIT)


def _const_spec(shape):
    return pl.BlockSpec(shape, lambda *_: (0,) * len(shape))


def _chunk_sum_matrix(rows):
    t = np.arange(rows)
    same = (t[:, None] // CHUNK) == (t[None, :] // CHUNK)
    return jnp.asarray(np.concatenate([same & (t[None, :] <= t[:, None]), same], axis=0), BF16)


def _norm_proj_kernel(x_ref, g_ref, *refs, splits):
    w_refs, o_refs = refs[:len(splits)], refs[len(splits):]
    h = _rms(x_ref[...], g_ref[...]).astype(BF16)
    outs = iter(o_refs)
    for w_ref, cols in zip(w_refs, splits):
        off = 0
        for n in cols:
            next(outs)[...] = jnp.dot(h, w_ref[:, off:off + n], preferred_element_type=F32)
            off += n


def _norm_proj(x, g, ws, splits, name):
    m, d = x.shape
    tm = PROJ_ROWS
    widths = [n for cols in splits for n in cols]
    return pl.pallas_call(
        functools.partial(_norm_proj_kernel, splits=splits),
        grid=(m // tm,),
        in_specs=[pl.BlockSpec((tm, d), lambda i: (i, 0)), _const_spec((1, d))] + [_const_spec(w.shape) for w in ws],
        out_specs=[pl.BlockSpec((tm, n), lambda i: (i, 0)) for n in widths],
        out_shape=[jax.ShapeDtypeStruct((m, n), F32) for n in widths],
        compiler_params=_params(("parallel",)),
        name=name,
    )(x, g.reshape(1, d), *ws)


def _gdn_kernel(qkvz_ref, ba_ref, cw_ref, alog_ref, dtb_ref, nw_ref, lcat_ref,
                o_ref, xp_ref, s_ref, qs_ref, k_ref, kb_ref, vb_ref, kbg_ref, qd_ref, gc_ref,
                gl_ref, us_ref, wq_ref, qkm_ref, kdt_ref):
    n_seq, rows = qkvz_ref.shape[0], qkvz_ref.shape[1]
    width = o_ref.shape[2]
    n_heads = width // A_HEAD_DIM
    tb = pl.program_id(1)

    @pl.when(tb == 0)
    def _():
        xp_ref[:, 0:SUBLANES, :] = jnp.zeros((n_seq, SUBLANES, xp_ref.shape[2]), F32)
        s_ref[...] = jnp.zeros(s_ref.shape, F32)

    def l2n(x):
        return x * lax.rsqrt(_head_sums(x * x, A_HEAD_DIM) + EPS)

    for q_i in range(n_seq):
        at = slice(q_i * rows, (q_i + 1) * rows)
        xp = xp_ref.at[q_i]
        xp[SUBLANES:SUBLANES + rows, :] = qkvz_ref[q_i, :, 0:3 * width]
        acc = cw_ref[A_CONV - 1:A_CONV, :] * _shifted_rows(xp, 0, rows)
        for back in range(1, A_CONV):
            acc = acc + cw_ref[A_CONV - 1 - back:A_CONV - back, :] * _shifted_rows(xp, back, rows)
        xp[0:SUBLANES, :] = xp[rows:rows + SUBLANES, :]
        act = _silu(acc)
        q, k, v = act[:, 0:width], act[:, width:2 * width], act[:, 2 * width:3 * width]

        ba = ba_ref[q_i]
        bg = jnp.where(_iota(ba.shape, 1) < n_heads, _sigmoid(ba),
                       -jnp.exp(alog_ref[...]) * _softplus(ba + dtb_ref[...]))
        ct = _dot(lcat_ref[...], bg, pb=3)
        beta_f = _head_expand(bg, 0, n_heads, A_HEAD_DIM)
        gc_f = _head_expand(ct[0:rows], n_heads, n_heads, A_HEAD_DIM)
        gl_f = _head_expand(ct[rows:2 * rows], n_heads, n_heads, A_HEAD_DIM)

        kn = l2n(k)
        qs = l2n(q) * (A_HEAD_DIM ** -0.5)
        kb = kn * beta_f
        e_gc = jnp.exp(gc_f)
        qs_ref[at, :] = qs
        k_ref[at, :] = kn
        kb_ref[at, :] = kb
        vb_ref[at, :] = v * beta_f
        kbg_ref[at, :] = kb * e_gc
        qd_ref[at, :] = qs * e_gc
        gc_ref[at, :] = gc_f
        gl_ref[at, :] = gl_f

    pair_w = 2 * A_HEAD_DIM
    assert A_HEAD_DIM == 2 * CHUNK
    n_pairs = width // pair_w
    n_chunks = rows // CHUNK
    strict, lower = _pair_masks()
    first_head = _iota((CHUNK, 2 * CHUNK), 1) < CHUNK
    same_head = (_iota((pair_w, pair_w), 0) < A_HEAD_DIM) == (_iota((pair_w, pair_w), 1) < A_HEAD_DIM)

    def ld(ref, c, p):
        return ref[c * CHUNK:(c + 1) * CHUNK, p * pair_w:(p + 1) * pair_w]

    keys = [(c, p) for c in range(n_seq * n_chunks) for p in range(n_pairs)]
    g = [_dot(jnp.concatenate([ld(kb_ref, *k), ld(qs_ref, *k)], axis=0), _stack_pair(ld(k_ref, *k)), NT)
         for k in keys]
    n_ms = []
    for i, k in enumerate(keys):
        gcp = ld(gc_ref, *k)
        g_t = jnp.where(first_head, gcp[:, 0:A_HEAD_DIM], gcp[:, A_HEAD_DIM:pair_w])
        g_s = jnp.concatenate([gcp[:, 0:A_HEAD_DIM], gcp[:, A_HEAD_DIM:pair_w]], axis=0).T[0:CHUNK]
        dec = jnp.exp(jnp.where(lower, g_t - g_s, 0.0))
        n_ms.append(jnp.where(strict, -(g[i][0:CHUNK] * dec), 0.0))
        qkm_ref[i] = jnp.where(lower, g[i][CHUNK:2 * CHUNK] * dec, 0.0).astype(BF16)
    t_ms = _inv_unit_lower(n_ms)
    for i, k in enumerate(keys):
        uw = _dot(t_ms[i], jnp.concatenate([_stack_pair(ld(vb_ref, *k)), _stack_pair(ld(kbg_ref, *k))], axis=1))
        us_ref[i] = uw[:, 0:pair_w]
        wq_ref[i] = jnp.concatenate([uw[:, pair_w:2 * pair_w], ld(qd_ref, *k)], axis=0).astype(BF16)
        kdt_ref[i] = (ld(k_ref, *k) * jnp.exp(ld(gl_ref, *k) - ld(gc_ref, *k))).T.astype(BF16)

    chains = [(q_i, p) for q_i in range(n_seq) for p in range(n_pairs)]
    s = [s_ref[q_i * n_pairs + p] for q_i, p in chains]
    for c in range(n_chunks):
        idx = [(q_i * n_chunks + c) * n_pairs + p for q_i, p in chains]
        x = [_dot(wq_ref[i], s[n]) for n, i in enumerate(idx)]
        v_new = [us_ref[i] - x[n][0:CHUNK] for n, i in enumerate(idx)]
        o_c = [x[n][CHUNK:2 * CHUNK] + _dot(qkm_ref[i], _stack_pair(v_new[n])) for n, i in enumerate(idx)]
        for n, ((q_i, p), i) in enumerate(zip(chains, idx)):
            r0 = (q_i * n_chunks + c) * CHUNK
            g_last = jnp.exp(gl_ref[r0:r0 + 1, p * pair_w:(p + 1) * pair_w])
            s[n] = s[n] * g_last + jnp.where(same_head, _dot(kdt_ref[i], v_new[n]), 0.0)
        for q_i in range(n_seq):
            o = jnp.concatenate(o_c[q_i * n_pairs:(q_i + 1) * n_pairs], axis=1)
            ms = _head_sums(o * o, A_HEAD_DIM) * (1.0 / A_HEAD_DIM)
            z = qkvz_ref[q_i, c * CHUNK:(c + 1) * CHUNK, 3 * width:4 * width]
            o_ref[q_i, c * CHUNK:(c + 1) * CHUNK, :] = (
                o * lax.rsqrt(ms + EPS) * nw_ref[...] * _silu(z)).astype(o_ref.dtype)
    for n in range(len(chains)):
        s_ref[n] = s[n]


def _seq_groups(x, n_seq):
    return x.reshape(n_seq, x.shape[0] // n_seq, x.shape[1])


def _gdn(qkvz, ba, conv_w, a_log, dt_bias, norm_w, batch):
    m = qkvz.shape[0]
    width = qkvz.shape[1] // 4
    n_heads = width // A_HEAD_DIM
    rows = GDN_ROWS
    n_seq = GDN_SEQS
    assert batch % n_seq == 0 and (m // batch) % rows == 0
    tblocks = m // batch // rows
    pad = lambda p: jnp.zeros((1, LANES), F32).at[0, n_heads:2 * n_heads].set(p)
    lcat = _chunk_sum_matrix(rows)
    row_spec = lambda w: pl.BlockSpec((n_seq, rows, w), lambda b, i: (0, b * tblocks + i, 0))
    act = lambda: pltpu.VMEM((n_seq * rows, width), F32)
    pair_w = 2 * A_HEAD_DIM
    n_pairs = width // pair_w
    n_keys = n_seq * (rows // CHUNK) * n_pairs
    out = pl.pallas_call(
        _gdn_kernel,
        grid=(batch // n_seq, tblocks),
        in_specs=[row_spec(4 * width), row_spec(LANES), _const_spec(conv_w.shape), _const_spec((1, LANES)),
                  _const_spec((1, LANES)), _const_spec((1, width)), _const_spec(lcat.shape)],
        out_specs=row_spec(width),
        out_shape=jax.ShapeDtypeStruct((n_seq, m // n_seq, width), BF16),
        scratch_shapes=[pltpu.VMEM((n_seq, rows + SUBLANES, 3 * width), F32),
                        pltpu.VMEM((n_seq * n_pairs, pair_w, pair_w), F32)]
                       + [act() for _ in range(8)]
                       + [pltpu.VMEM((n_keys, CHUNK, pair_w), F32), pltpu.VMEM((n_keys, 2 * CHUNK, pair_w), BF16),
                          pltpu.VMEM((n_keys, CHUNK, 2 * CHUNK), BF16), pltpu.VMEM((n_keys, pair_w, CHUNK), BF16)],
        compiler_params=_params(("arbitrary", "arbitrary")),
        name="gdn_mixer",
    )(_seq_groups(qkvz, n_seq), _seq_groups(ba, n_seq), conv_w, pad(a_log), pad(dt_bias),
      jnp.tile(norm_w, n_heads).reshape(1, width), lcat)
    return out.reshape(m, width)


def _rwkv_kernel(pb_ref, mu_ref, w0_ref, w2_ref, a0_ref, a2_ref, g2_ref, kk_ref, ka_ref, rk_ref, lnw_ref,
                 lnb_ref, lcat_ref, o_ref, xp_ref, s_ref, at_ref, rt_ref, bt_ref, kt_ref,
                 v_ref, bv_ref, km_ref, cum_ref, tot_ref, r_ref, gate_ref, lx_ref, tw_ref, mc_ref, bkt_ref,
                 pcol_ref):
    n_seq, rows = pb_ref.shape[0], pb_ref.shape[1]
    width = o_ref.shape[2]
    tb = pl.program_id(1)

    @pl.when(tb == 0)
    def _():
        xp_ref[:, 0:SUBLANES, :] = jnp.zeros((n_seq, SUBLANES, xp_ref.shape[2]), F32)
        s_ref[...] = jnp.zeros(s_ref.shape, F32)

    def gsum(x):
        return _head_sums(x, B_HEAD_DIM)

    for q_i in range(n_seq):
        at = slice(q_i * rows, (q_i + 1) * rows)
        xp = xp_ref.at[q_i]
        p = pb_ref[q_i]
        xp[SUBLANES:SUBLANES + rows, :] = p
        prev = _shifted_rows(xp, 1, rows)
        xp[0:SUBLANES, :] = xp[rows:rows + SUBLANES, :]
        m = p + (prev - p) * mu_ref[...]
        r, kr, vr = m[:, 0:width], m[:, width:2 * width], m[:, 2 * width:3 * width]
        o1 = 3 * width
        o2 = o1 + w2_ref.shape[0]
        o3 = o2 + a2_ref.shape[0]
        w_lo, a_lo, g_lo = m[:, o1:o2], m[:, o2:o3], m[:, o3:]

        lw = -math.exp(-0.5) * _sigmoid(w0_ref[...] + _dot(jnp.tanh(w_lo), w2_ref[...]))
        a_lr = _sigmoid(a0_ref[...] + _dot(a_lo, a2_ref[...]))
        gate_ref[at, :] = _dot(_sigmoid(g_lo), g2_ref[...])

        kk = kr * kk_ref[...]
        k_mod = kr * (1.0 + (a_lr - 1.0) * ka_ref[...])
        kk = kk * lax.rsqrt(gsum(kk * kk) + EPS)
        a_vec = -kk
        b_vec = kk * a_lr

        ct = _dot(lcat_ref[...], lw, pb=2)
        cum, tot = ct[0:rows], ct[rows:2 * rows]
        e_neg = jnp.exp(-cum)
        rt_ref[at, :] = r * jnp.exp(cum)
        at_ref[at, :] = a_vec * jnp.exp(cum - lw)
        bt_ref[at, :] = b_vec * e_neg
        kt_ref[at, :] = k_mod * e_neg
        bv_ref[at, :] = b_vec
        km_ref[at, :] = k_mod
        cum_ref[at, :] = cum
        tot_ref[at, :] = tot
        r_ref[at, :] = r
        v_ref[at, :] = vr

    pair_w = 2 * B_HEAD_DIM
    n_pairs = width // pair_w
    n_chunks = rows // CHUNK
    strict, lower = _pair_masks()
    lower2 = jnp.concatenate([lower, lower], axis=1)
    same_head = (_iota((pair_w, pair_w), 0) < B_HEAD_DIM) == (_iota((pair_w, pair_w), 1) < B_HEAD_DIM)

    def ld(ref, c, pr):
        return ref[c * CHUNK:(c + 1) * CHUNK, pr * pair_w:(pr + 1) * pair_w]

    for q_i in range(n_seq):
        keys = [(q_i * n_chunks + c, pr) for c in range(n_chunks) for pr in range(n_pairs)]
        g = [_dot(jnp.concatenate([ld(at_ref, *k), ld(rt_ref, *k)], axis=0),
                  jnp.concatenate([_stack_pair(ld(bt_ref, *k)), _stack_pair(ld(kt_ref, *k))], axis=0), NT)
             for k in keys]
        t_ms = _inv_unit_lower([jnp.where(strict, gi[0:CHUNK, 0:2 * CHUNK], 0.0) for gi in g])
        w1 = [_dot(jnp.where(strict, g[i][0:CHUNK, 2 * CHUNK:4 * CHUNK], 0.0), _stack_pair(ld(v_ref, *k)))
              for i, k in enumerate(keys)]
        taw = [_dot(t_ms[i], jnp.concatenate([_stack_pair(ld(at_ref, *k)), _stack_pair(w1[i])], axis=1))
               for i, k in enumerate(keys)]
        for i, k in enumerate(keys):
            j = k[0] * n_pairs + k[1]
            lx_ref[j] = jnp.concatenate([taw[i][:, 0:pair_w], ld(rt_ref, *k)], axis=0).astype(BF16)
            tw_ref[j] = taw[i][:, pair_w:2 * pair_w]
            mc_ref[j] = jnp.where(lower2, g[i][CHUNK:2 * CHUNK], 0.0).astype(BF16)
            tot_c = ld(tot_ref, *k)
            e_dec = jnp.exp(tot_c - ld(cum_ref, *k))
            bkt_ref[j] = jnp.concatenate([ld(bv_ref, *k) * e_dec, ld(km_ref, *k) * e_dec], axis=0).T.astype(BF16)
            pcol_ref[j] = jnp.broadcast_to(jnp.exp(tot_c[0:SUBLANES].T[:, 0:1]), (pair_w, pair_w))

    chains = [(q_i, pr) for q_i in range(n_seq) for pr in range(n_pairs)]
    s = [s_ref[q_i * n_pairs + pr] for q_i, pr in chains]
    for c in range(n_chunks):
        gc = [q_i * n_chunks + c for q_i, _ in chains]
        idx = [gc[n] * n_pairs + pr for n, (_, pr) in enumerate(chains)]
        v_c = [ld(v_ref, gc[n], pr) for n, (_, pr) in enumerate(chains)]
        x = [_dot(lx_ref[j], s[n]) for n, j in enumerate(idx)]
        u = [x[n][0:CHUNK] + tw_ref[j] for n, j in enumerate(idx)]
        y_c = [x[n][CHUNK:2 * CHUNK]
               + _dot(mc_ref[j], jnp.concatenate([_stack_pair(u[n]), _stack_pair(v_c[n])], axis=0))
               for n, j in enumerate(idx)]
        for n, j in enumerate(idx):
            upd = _dot(bkt_ref[j], jnp.concatenate([u[n], v_c[n]], axis=0))
            s[n] = s[n] * pcol_ref[j] + jnp.where(same_head, upd, 0.0)
        for q_i in range(n_seq):
            rows_c = slice((q_i * n_chunks + c) * CHUNK, (q_i * n_chunks + c + 1) * CHUNK)
            y = jnp.concatenate(y_c[q_i * n_pairs:(q_i + 1) * n_pairs], axis=1)
            inv_n = 1.0 / B_HEAD_DIM
            mu = gsum(y) * inv_n
            yc = y - mu
            var = gsum(yc * yc) * inv_n
            yn = yc * lax.rsqrt(var + B_LN_EPS) * lnw_ref[...] + lnb_ref[...]
            bonus = gsum(r_ref[rows_c, :] * km_ref[rows_c, :] * rk_ref[...]) * v_ref[rows_c, :]
            o_ref[q_i, c * CHUNK:(c + 1) * CHUNK, :] = ((yn + bonus) * gate_ref[rows_c, :]).astype(o_ref.dtype)
    for n in range(len(chains)):
        s_ref[n] = s[n]


def _rwkv(pb, mu, w0, w2, a0, a2, g2, k_k, k_a, r_k, ln_w, ln_b, batch):
    m, in_w = pb.shape
    width = w0.shape[0]
    rows = MIX_ROWS
    n_seq = MIX_SEQS
    assert batch % n_seq == 0 and (m // batch) % rows == 0
    tblocks = m // batch // rows
    lcat = _chunk_sum_matrix(rows)
    row = lambda p: p.reshape(1, -1).astype(F32)
    row_spec = lambda w: pl.BlockSpec((n_seq, rows, w), lambda b, i: (0, b * tblocks + i, 0))
    vec = _const_spec((1, width))
    act = lambda: pltpu.VMEM((n_seq * rows, width), F32)
    pair_w = 2 * B_HEAD_DIM
    n_pairs = width // pair_w
    n_keys = n_seq * (rows // CHUNK) * n_pairs
    out = pl.pallas_call(
        _rwkv_kernel,
        grid=(batch // n_seq, tblocks),
        in_specs=[row_spec(in_w), _const_spec((1, in_w)), vec, _const_spec(w2.shape), vec,
                  _const_spec(a2.shape), _const_spec(g2.shape), vec, vec, vec, vec, vec,
                  _const_spec(lcat.shape)],
        out_specs=row_spec(width),
        out_shape=jax.ShapeDtypeStruct((n_seq, m // n_seq, width), BF16),
        scratch_shapes=[pltpu.VMEM((n_seq, rows + SUBLANES, in_w), F32),
                        pltpu.VMEM((n_seq * n_pairs, pair_w, pair_w), F32)]
                       + [act() for _ in range(11)]
                       + [pltpu.VMEM((n_keys, 2 * CHUNK, pair_w), BF16), pltpu.VMEM((n_keys, CHUNK, pair_w), F32),
                          pltpu.VMEM((n_keys, CHUNK, 4 * CHUNK), BF16), pltpu.VMEM((n_keys, pair_w, 2 * CHUNK), BF16),
                          pltpu.VMEM((n_keys, pair_w, pair_w), F32)],
        compiler_params=_params(("arbitrary", "arbitrary")),
        name="rwkv_mixer",
    )(_seq_groups(pb, n_seq), row(mu), row(w0), w2, row(a0), a2, g2, row(k_k), row(k_a), row(r_k), row(ln_w),
      row(ln_b), lcat)
    return out.reshape(m, width)


def _rglru_body(gx_ref, cw_ref, cb_ref, wa_ref, ba_ref, wx_ref, bx_ref, lam_ref, o_ref, xp_ref, h_ref):
    rows = gx_ref.shape[0]
    width = o_ref.shape[1]
    xp_ref[SUBLANES:SUBLANES + rows, :] = gx_ref[:, width:2 * width]
    xc = cb_ref[...] + cw_ref[C_CONV - 1:C_CONV, :] * _shifted_rows(xp_ref, 0, rows)
    for back in range(1, C_CONV):
        xc = xc + cw_ref[C_CONV - 1 - back:C_CONV - back, :] * _shifted_rows(xp_ref, back, rows)
    xp_ref[0:SUBLANES, :] = xp_ref[rows:rows + SUBLANES, :]

    gate_r = _sigmoid(_dot(xc, wa_ref[...]) + ba_ref[...])
    gate_i = _sigmoid(_dot(xc, wx_ref[...]) + bx_ref[...])
    log_a = -C_GATE_SCALE * gate_r * _softplus(-lam_ref[...])
    a = jnp.exp(log_a)
    d = jnp.sqrt(jnp.tanh(-log_a) * (a * a + 1.0)) * (gate_i * xc)

    t_idx = _iota((rows, width), 0)
    shift = 1
    while shift < SUBLANES:
        keep = t_idx >= shift
        d = jnp.where(keep, a * pltpu.roll(d, shift, 0) + d, d)
        a = jnp.where(keep, a * pltpu.roll(a, shift, 0), a)
        shift *= 2
    while shift < rows:
        d = jnp.concatenate([d[:shift], a[shift:] * d[:rows - shift] + d[shift:]], axis=0)
        a = jnp.concatenate([a[:shift], a[shift:] * a[:rows - shift]], axis=0)
        shift *= 2
    h = a * h_ref[0:1, :] + d
    h_ref[...] = jnp.broadcast_to(h[rows - 1:rows, :], h_ref.shape)
    o_ref[...] = (h * _gelu_tanh(gx_ref[:, 0:width])).astype(o_ref.dtype)


_LOG_GAMMA = tuple(math.log(1.0 - 2.0 ** (-5.0 - h)) for h in range(D_HEADS))


def _per_head(lane_head, values):
    out = jnp.full(lane_head.shape, values[-1], F32)
    for h in range(len(values) - 2, -1, -1):
        out = jnp.where(lane_head == h, values[h], out)
    return out


def _retention_body(qk_ref, vg_ref, cos_ref, sin_ref, gw_ref, gb_ref, o_ref, s_ref, dm_ref):
    rows = qk_ref.shape[0]
    kw = qk_ref.shape[1] // 2
    vw = o_ref.shape[1]

    def rotary(x):
        even = _mod(_iota((rows, LANES), 1), 2) == 0
        parts = []
        for c in range(kw // LANES):
            xc = x[:, c * LANES:(c + 1) * LANES]
            parts.append(jnp.where(even, pltpu.roll(xc, LANES - 1, 1), pltpu.roll(xc, 1, 1)))
        return x * cos_ref[...] + jnp.concatenate(parts, axis=1) * sin_ref[...]

    q = rotary(qk_ref[:, 0:kw])
    k = rotary(qk_ref[:, kw:2 * kw]) * (D_KEY_DIM ** -0.5)
    v = vg_ref[:, 0:vw]
    k_head = _div(_iota((1, kw), 1), D_KEY_DIM)
    lg_k = _per_head(k_head, _LOG_GAMMA)
    lg_v = _per_head(_div(_iota((1, vw), 1), D_VAL_DIM), _LOG_GAMMA)
    t_idx = _iota((rows, 1), 0).astype(F32)

    o_cross = _dot(q * jnp.exp((t_idx + 1.0) * lg_k), s_ref[...])
    inner = []
    for h in range(D_HEADS):
        scores = _dot(jnp.where(k_head == h, q, 0.0), k, NT) * dm_ref[h]
        inner.append(_dot(scores, v[:, h * D_VAL_DIM:(h + 1) * D_VAL_DIM]))
    o = o_cross + jnp.concatenate(inner, axis=1)

    k_dec = k * jnp.exp((rows - 1.0 - t_idx) * lg_k)
    same_head = _div(_iota((kw, vw), 0), D_KEY_DIM) == _div(_iota((kw, vw), 1), D_VAL_DIM)
    s_ref[...] = s_ref[...] * jnp.exp(float(rows) * lg_v) + jnp.where(same_head, _dot(k_dec, v, TN), 0.0)

    inv_n = 1.0 / D_VAL_DIM
    mu = _head_sums(o, D_VAL_DIM) * inv_n
    oc = o - mu
    var = _head_sums(oc * oc, D_VAL_DIM) * inv_n
    on = oc * lax.rsqrt(var + EPS) * gw_ref[...] + gb_ref[...]
    o_ref[...] = (_silu(vg_ref[:, vw:2 * vw]) * on).astype(o_ref.dtype)


def _mixer_cd_kernel(gx_ref, cw_ref, cb_ref, wa_ref, ba_ref, wx_ref, bx_ref, lam_ref, qk_ref, vg_ref, cos_ref,
                     sin_ref, gw_ref, gb_ref, oc_ref, od_ref, xp_ref, h_ref, s_ref, dm_ref):
    rows = qk_ref.shape[0]

    @pl.when((pl.program_id(0) == 0) & (pl.program_id(1) == 0))
    def _():
        rel = (_iota((rows, rows), 0) - _iota((rows, rows), 1)).astype(F32)
        for h in range(D_HEADS):
            dm_ref[h] = jnp.where(rel >= 0, jnp.exp(jnp.maximum(rel, 0.0) * _LOG_GAMMA[h]), 0.0)

    @pl.when(pl.program_id(1) == 0)
    def _():
        xp_ref[0:SUBLANES, :] = jnp.zeros((SUBLANES, xp_ref.shape[1]), F32)
        h_ref[...] = jnp.zeros(h_ref.shape, F32)
        s_ref[...] = jnp.zeros(s_ref.shape, F32)

    _retention_body(qk_ref, vg_ref, cos_ref, sin_ref, gw_ref, gb_ref, od_ref, s_ref, dm_ref)
    _rglru_body(gx_ref, cw_ref, cb_ref, wa_ref, ba_ref, wx_ref, bx_ref, lam_ref, oc_ref, xp_ref, h_ref)


def _mixer_cd(gx, qk, vg, conv_w, conv_b, wa, ba, wx, bx, lam, gn_w, gn_b, batch):
    m = qk.shape[0]
    cw = gx.shape[1] // 2
    kw = qk.shape[1] // 2
    vw = vg.shape[1] // 2
    rows = MIX_ROWS
    seq = m // batch
    tblocks = seq // rows
    inv = 1.0 / (ROPE_BASE ** jnp.linspace(0.0, 1.0, D_KEY_DIM // 2, dtype=F32))
    ang = jnp.arange(seq).astype(F32)[:, None] * inv[None, :]
    cos = jnp.tile(jnp.repeat(jnp.cos(ang), 2, axis=1), (1, kw // D_KEY_DIM))
    sin = jnp.tile(jnp.stack([-jnp.sin(ang), jnp.sin(ang)], axis=-1).reshape(seq, D_KEY_DIM),
                   (1, kw // D_KEY_DIM))
    row = lambda p: p.reshape(1, -1).astype(F32)
    blockdiag = lambda w: jax.scipy.linalg.block_diag(*[w[i] for i in range(w.shape[0])]).astype(BF16)
    row_spec = lambda w: pl.BlockSpec((rows, w), lambda b, i: (b * tblocks + i, 0))
    tab_spec = pl.BlockSpec((rows, kw), lambda b, i: (i, 0))
    cvec, cmat, vvec = _const_spec((1, cw)), _const_spec((cw, cw)), _const_spec((1, vw))
    return pl.pallas_call(
        _mixer_cd_kernel,
        grid=(batch, tblocks),
        in_specs=[row_spec(2 * cw), _const_spec(conv_w.shape), cvec, cmat, cvec, cmat, cvec, cvec,
                  row_spec(2 * kw), row_spec(2 * vw), tab_spec, tab_spec, vvec, vvec],
        out_specs=[row_spec(cw), row_spec(vw)],
        out_shape=[jax.ShapeDtypeStruct((m, cw), BF16), jax.ShapeDtypeStruct((m, vw), BF16)],
        scratch_shapes=[pltpu.VMEM((rows + SUBLANES, cw), F32), pltpu.VMEM((SUBLANES, cw), F32),
                        pltpu.VMEM((kw, vw), F32), pltpu.VMEM((D_HEADS, rows, rows), F32)],
        compiler_params=_params(("arbitrary", "arbitrary")),
        name="rglru_retention_mixer",
    )(gx, conv_w, row(conv_b), blockdiag(wa), row(ba), blockdiag(wx), row(bx), row(lam),
      qk, vg, cos, sin, row(gn_w), row(gn_b))


def _ffn_kernel(x_ref, xh_ref, oa_ref, oah_ref, ob_ref, obh_ref, wo_ref, g_ref, wup_ref, wc_ref, wd_ref,
                *rest, seq_blocks, final):
    if final:
        gf_ref, y_ref, hc_ref, u_ref = rest
    else:
        y_ref, hc_ref, u_ref = rest
    rows = x_ref.shape[0]
    half = oa_ref.shape[1]
    ffn = wd_ref.shape[0]

    def mixed(x, oa, ob):
        return (x + jnp.dot(oa, wo_ref[0:half, :], preferred_element_type=F32)
                + jnp.dot(ob, wo_ref[half:2 * half, :], preferred_element_type=F32))

    x1 = mixed(x_ref[...], oa_ref[...], ob_ref[...])
    x1h = mixed(xh_ref[...], oah_ref[...], obh_ref[...])
    starts_seq = (pl.program_id(0) % seq_blocks) == 0
    hc_ref[0:BF16_ROWS, :] = (_rms(x1h, g_ref[...]) * jnp.where(starts_seq, 0.0, 1.0)).astype(BF16)
    hc_ref[BF16_ROWS:BF16_ROWS + rows, :] = _rms(x1, g_ref[...]).astype(BF16)

    u_ref[...] = jnp.dot(hc_ref[...], wup_ref[:, 0:ffn], preferred_element_type=F32)
    v = jnp.dot(hc_ref[BF16_ROWS:BF16_ROWS + rows, :], wup_ref[:, ffn:2 * ffn], preferred_element_type=F32)
    uc = wc_ref[FFN_CONV - 1:FFN_CONV, :] * u_ref[pl.ds(BF16_ROWS, rows), :]
    for back in range(1, FFN_CONV):
        uc = uc + wc_ref[FFN_CONV - 1 - back:FFN_CONV - back, :] * u_ref[pl.ds(BF16_ROWS - back, rows), :]
    gl = (_silu(uc) * v).astype(BF16)
    y = x1 + jnp.dot(gl, wd_ref[...], preferred_element_type=F32)
    if final:
        y = _rms(y, gf_ref[...])
    y_ref[...] = y


def _ffn(x, oa, ob, w_out, norm_w, w_up, w_conv, w_down, seq, final_norm=None):
    m, d = x.shape
    half = oa.shape[1]
    ffn = w_down.shape[0]
    tm = FFN_ROWS
    halo = BF16_ROWS
    final = final_norm is not None
    row_spec = lambda w: pl.BlockSpec((tm, w), lambda i: (i, 0))
    halo_spec = lambda w: pl.BlockSpec((halo, w), lambda i: (jnp.maximum(i * (tm // halo) - 1, 0), 0))
    resident = lambda w: pl.BlockSpec(w.shape, lambda i: (0, 0), pipeline_mode=pl.Buffered(1))
    row = lambda p: p.reshape(1, -1).astype(F32)
    in_specs = [row_spec(d), halo_spec(d), row_spec(half), halo_spec(half), row_spec(half), halo_spec(half),
                resident(w_out), _const_spec((1, d)), resident(w_up), _const_spec(w_conv.shape), resident(w_down)]
    args = [x, x, oa, oa, ob, ob, w_out, row(norm_w), w_up, w_conv, w_down]
    if final:
        in_specs.append(_const_spec((1, d)))
        args.append(row(final_norm))
    return pl.pallas_call(
        functools.partial(_ffn_kernel, seq_blocks=seq // tm, final=final),
        grid=(m // tm,),
        in_specs=in_specs,
        out_specs=row_spec(d),
        out_shape=jax.ShapeDtypeStruct((m, d), F32),
        scratch_shapes=[pltpu.VMEM((tm + halo, d), BF16), pltpu.VMEM((tm + halo, ffn), F32)],
        compiler_params=_params(("parallel",)),
        name="outproj_ffn_final" if final else "outproj_ffn",
    )(*args)


def kernel(x, l0_norm1, l0_w_in, l0_a_conv, l0_a_A_log, l0_a_dt_bias, l0_a_norm, l0_b_mu, l0_b_w0, l0_b_w2, l0_b_a0, l0_b_a2, l0_b_g2, l0_b_k_k, l0_b_k_a, l0_b_r_k, l0_b_ln_w, l0_b_ln_b, l0_w_out, l0_norm2, l0_ffn_up, l0_ffn_conv, l0_ffn_down, l1_norm1, l1_w_in, l1_c_conv_w, l1_c_conv_b, l1_c_wa, l1_c_ba, l1_c_wx, l1_c_bx, l1_c_lambda, l1_d_gn_w, l1_d_gn_b, l1_w_out, l1_norm2, l1_ffn_up, l1_ffn_conv, l1_ffn_down, final_norm):
    batch, seq, d = x.shape
    xf = x.reshape(batch * seq, d)
    bf = lambda w: w.astype(BF16)

    a_w = l0_a_conv.shape[1] // 3
    a_heads = l0_a_A_log.shape[0]
    a_in = 4 * a_w + 2 * a_heads
    w_gates = jnp.concatenate([bf(l0_w_in[:, 4 * a_w:a_in]), jnp.zeros((d, LANES - 2 * a_heads), BF16)], axis=1)
    b_in = l0_w_in.shape[1] - a_in
    qkvz, ba, pb = _norm_proj(xf, l0_norm1, [bf(l0_w_in[:, 0:4 * a_w]), w_gates, bf(l0_w_in[:, a_in:])],
                              ((4 * a_w,), (LANES,), (b_in,)), "norm_proj0")
    o_a = _gdn(qkvz, ba, l0_a_conv, l0_a_A_log, l0_a_dt_bias, l0_a_norm, batch)
    o_bb = _rwkv(pb, l0_b_mu, l0_b_w0, bf(l0_b_w2), l0_b_a0, bf(l0_b_a2), bf(l0_b_g2),
                 l0_b_k_k, l0_b_k_a, l0_b_r_k, l0_b_ln_w, l0_b_ln_b, batch)
    x1 = _ffn(xf, o_a, o_bb, bf(l0_w_out), l0_norm2, bf(l0_ffn_up), l0_ffn_conv, bf(l0_ffn_down), seq)

    c_w = l1_c_lambda.shape[0]
    d_w = l1_d_gn_w.shape[0]
    qk_w = l1_w_in.shape[1] - 2 * c_w - 2 * d_w
    gx, qk, vg = _norm_proj(x1, l1_norm1, [bf(l1_w_in)], ((2 * c_w, qk_w, 2 * d_w),), "norm_proj1")
    o_c, o_d = _mixer_cd(gx, qk, vg, l1_c_conv_w, l1_c_conv_b, l1_c_wa, l1_c_ba, l1_c_wx, l1_c_bx, l1_c_lambda,
                         l1_d_gn_w, l1_d_gn_b, batch)
    y = _ffn(x1, o_c, o_d, bf(l1_w_out), l1_norm2, bf(l1_ffn_up), l1_ffn_conv, bf(l1_ffn_down), seq,
             final_norm=final_norm)
    return y.reshape(batch, seq, d)
```

```python
import functools
import math

import numpy as np
import jax
import jax.numpy as jnp
from jax import lax
from jax.experimental import pallas as pl
from jax.experimental.pallas import tpu as pltpu

F32 = jnp.float32
BF16 = jnp.bfloat16

EPS = 1e-6
CHUNK = 64
A_HEAD_DIM = 128
A_CONV = 4
B_HEAD_DIM = 64
B_LN_EPS = 64e-5
C_GATE_SCALE = 8.0
C_CONV = 4
D_KEY_DIM = 64
D_VAL_DIM = 128
D_HEADS = 4
ROPE_BASE = 10000.0
FFN_CONV = 3

LANES = 128
SUBLANES = 8
BF16_ROWS = 16
VMEM_LIMIT = 56 * 1024 * 1024

MIX_ROWS = 256
MIX_SEQS = 2
GDN_ROWS = 128
GDN_SEQS = 4
PROJ_ROWS = 512
FFN_ROWS = 512

NN = (((1,), (0,)), ((), ()))
NT = (((1,), (1,)), ((), ()))
TN = (((0,), (0,)), ((), ()))


def _pieces(x, n):
    if x.dtype == BF16:
        return [x]
    out, r = [], x
    for i in range(n):
        p = r.astype(BF16)
        out.append(p)
        if i + 1 < n:
            r = r - p.astype(F32)
    return out


def _dot(a, b, dims=NN, pa=1, pb=1):
    ap, bp = _pieces(a, pa), _pieces(b, pb)
    order = max(len(ap), len(bp))
    acc = None
    for i in reversed(range(len(ap))):
        for j in reversed(range(len(bp))):
            if i + j < order:
                t = lax.dot_general(ap[i], bp[j], dims, preferred_element_type=F32)
                acc = t if acc is None else acc + t
    return acc


def _sigmoid(x):
    return 0.5 * jnp.tanh(0.5 * x) + 0.5


def _silu(x):
    return x * _sigmoid(x)


def _softplus(x):
    return jnp.maximum(x, 0.0) + jnp.log1p(jnp.exp(-jnp.abs(x)))


def _gelu_tanh(x):
    return 0.5 * x * (1.0 + jnp.tanh(math.sqrt(2.0 / math.pi) * (x + 0.044715 * (x * x * x))))


def _rms(x, g):
    return x * lax.rsqrt(jnp.mean(x * x, axis=-1, keepdims=True) + EPS) * g


def _iota(shape, dim):
    return lax.broadcasted_iota(jnp.int32, shape, dim)


def _div(i, n):
    assert n & (n - 1) == 0
    return i >> (n.bit_length() - 1)


def _mod(i, n):
    assert n & (n - 1) == 0
    return i & (n - 1)


def _pair_masks():
    t, s = _iota((CHUNK, 2 * CHUNK), 0), _mod(_iota((CHUNK, 2 * CHUNK), 1), CHUNK)
    return s < t, s <= t


def _stack_heads(x, n):
    head = _div(_iota(x.shape, 1), x.shape[1] // n)
    return jnp.concatenate([jnp.where(head == h, x, 0.0) for h in range(n)], axis=0)


def _stack_pair(x):
    return _stack_heads(x, 2)


def _inv_unit_lower(n_mats):
    shape = n_mats[0].shape
    heads = shape[1] // CHUNK
    stack = functools.partial(_stack_heads, n=heads)
    eye = (_mod(_iota(shape, 1), CHUNK) == _iota(shape, 0)).astype(F32)
    ts = [eye + n for n in n_mats]
    ps = [_dot(n, stack(n)) for n in n_mats]
    levels = int(math.log2(CHUNK)) - 1
    for level in range(levels):
        last = level == levels - 1
        lhs = ts if last else [jnp.concatenate([t, p], axis=0) for t, p in zip(ts, ps)]
        prod = [_dot(l, stack(p)) for l, p in zip(lhs, ps)]
        ts = [t + pr[0:CHUNK] for t, pr in zip(ts, prod)]
        if not last:
            ps = [pr[CHUNK:2 * CHUNK] for pr in prod]
    return ts


def _head_sums(x, dim):
    assert dim in (LANES, LANES // 2)
    low = _iota((x.shape[0], LANES), 1) < dim
    parts = []
    for c in range(x.shape[1] // LANES):
        xc = x[:, c * LANES:(c + 1) * LANES]
        total = jnp.sum(xc, axis=-1, keepdims=True)
        if dim == LANES:
            parts.append(jnp.broadcast_to(total, xc.shape))
        else:
            first = jnp.sum(jnp.where(low, xc, 0.0), axis=-1, keepdims=True)
            parts.append(jnp.where(low, first, total - first))
    return jnp.concatenate(parts, axis=1)


def _head_expand(cols, first, heads, dim):
    return jnp.concatenate([jnp.broadcast_to(cols[:, first + h:first + h + 1], (cols.shape[0], dim))
                            for h in range(heads)], axis=1)


def _shifted_rows(xp_ref, back, rows):
    if back == 0:
        return xp_ref[SUBLANES:SUBLANES + rows, :]
    return pltpu.roll(xp_ref[...], back, 0)[SUBLANES:SUBLANES + rows, :]


def _params(sem):
    return pltpu.CompilerParams(dimension_semantics=sem, vmem_limit_bytes=VMEM_LIMIT)


def _const_spec(shape):
    return pl.BlockSpec(shape, lambda *_: (0,) * len(shape))


def _chunk_sum_matrix(rows):
    t = np.arange(rows)
    same = (t[:, None] // CHUNK) == (t[None, :] // CHUNK)
    return jnp.asarray(np.concatenate([same & (t[None, :] <= t[:, None]), same], axis=0), BF16)


def _norm_proj_kernel(x_ref, g_ref, *refs, splits):
    w_refs, o_refs = refs[:len(splits)], refs[len(splits):]
    h = _rms(x_ref[...], g_ref[...]).astype(BF16)
    outs = iter(o_refs)
    for w_ref, cols in zip(w_refs, splits):
        off = 0
        for n in cols:
            next(outs)[...] = jnp.dot(h, w_ref[:, off:off + n], preferred_element_type=F32)
            off += n


def _norm_proj(x, g, ws, splits, name):
    m, d = x.shape
    tm = PROJ_ROWS
    widths = [n for cols in splits for n in cols]
    return pl.pallas_call(
        functools.partial(_norm_proj_kernel, splits=splits),
        grid=(m // tm,),
        in_specs=[pl.BlockSpec((tm, d), lambda i: (i, 0)), _const_spec((1, d))] + [_const_spec(w.shape) for w in ws],
        out_specs=[pl.BlockSpec((tm, n), lambda i: (i, 0)) for n in widths],
        out_shape=[jax.ShapeDtypeStruct((m, n), F32) for n in widths],
        compiler_params=_params(("parallel",)),
        name=name,
    )(x, g.reshape(1, d), *ws)


def _gdn_kernel(qkvz_ref, ba_ref, cw_ref, alog_ref, dtb_ref, nw_ref, lcat_ref,
                o_ref, xp_ref, s_ref, qs_ref, k_ref, kb_ref, vb_ref, kbg_ref, qd_ref, gc_ref,
                gl_ref, us_ref, wq_ref, qkm_ref, kdt_ref):
    n_seq, rows = qkvz_ref.shape[0], qkvz_ref.shape[1]
    width = o_ref.shape[2]
    n_heads = width // A_HEAD_DIM
    tb = pl.program_id(1)

    @pl.when(tb == 0)
    def _():
        xp_ref[:, 0:SUBLANES, :] = jnp.zeros((n_seq, SUBLANES, xp_ref.shape[2]), F32)
        s_ref[...] = jnp.zeros(s_ref.shape, F32)

    def l2n(x):
        return x * lax.rsqrt(_head_sums(x * x, A_HEAD_DIM) + EPS)

    for q_i in range(n_seq):
        at = slice(q_i * rows, (q_i + 1) * rows)
        xp = xp_ref.at[q_i]
        xp[SUBLANES:SUBLANES + rows, :] = qkvz_ref[q_i, :, 0:3 * width]
        acc = cw_ref[A_CONV - 1:A_CONV, :] * _shifted_rows(xp, 0, rows)
        for back in range(1, A_CONV):
            acc = acc + cw_ref[A_CONV - 1 - back:A_CONV - back, :] * _shifted_rows(xp, back, rows)
        xp[0:SUBLANES, :] = xp[rows:rows + SUBLANES, :]
        act = _silu(acc)
        q, k, v = act[:, 0:width], act[:, width:2 * width], act[:, 2 * width:3 * width]

        ba = ba_ref[q_i]
        bg = jnp.where(_iota(ba.shape, 1) < n_heads, _sigmoid(ba),
                       -jnp.exp(alog_ref[...]) * _softplus(ba + dtb_ref[...]))
        ct = _dot(lcat_ref[...], bg, pb=3)
        beta_f = _head_expand(bg, 0, n_heads, A_HEAD_DIM)
        gc_f = _head_expand(ct[0:rows], n_heads, n_heads, A_HEAD_DIM)
        gl_f = _head_expand(ct[rows:2 * rows], n_heads, n_heads, A_HEAD_DIM)

        kn = l2n(k)
        qs = l2n(q) * (A_HEAD_DIM ** -0.5)
        kb = kn * beta_f
        e_gc = jnp.exp(gc_f)
        qs_ref[at, :] = qs
        k_ref[at, :] = kn
        kb_ref[at, :] = kb
        vb_ref[at, :] = v * beta_f
        kbg_ref[at, :] = kb * e_gc
        qd_ref[at, :] = qs * e_gc
        gc_ref[at, :] = gc_f
        gl_ref[at, :] = gl_f

    pair_w = 2 * A_HEAD_DIM
    assert A_HEAD_DIM == 2 * CHUNK
    n_pairs = width // pair_w
    n_chunks = rows // CHUNK
    strict, lower = _pair_masks()
    first_head = _iota((CHUNK, 2 * CHUNK), 1) < CHUNK
    same_head = (_iota((pair_w, pair_w), 0) < A_HEAD_DIM) == (_iota((pair_w, pair_w), 1) < A_HEAD_DIM)

    def ld(ref, c, p):
        return ref[c * CHUNK:(c + 1) * CHUNK, p * pair_w:(p + 1) * pair_w]

    keys = [(c, p) for c in range(n_seq * n_chunks) for p in range(n_pairs)]
    g = [_dot(jnp.concatenate([ld(kb_ref, *k), ld(qs_ref, *k)], axis=0), _stack_pair(ld(k_ref, *k)), NT)
         for k in keys]
    n_ms = []
    for i, k in enumerate(keys):
        gcp = ld(gc_ref, *k)
        g_t = jnp.where(first_head, gcp[:, 0:A_HEAD_DIM], gcp[:, A_HEAD_DIM:pair_w])
        g_s = jnp.concatenate([gcp[:, 0:A_HEAD_DIM], gcp[:, A_HEAD_DIM:pair_w]], axis=0).T[0:CHUNK]
        dec = jnp.exp(jnp.where(lower, g_t - g_s, 0.0))
        n_ms.append(jnp.where(strict, -(g[i][0:CHUNK] * dec), 0.0))
        qkm_ref[i] = jnp.where(lower, g[i][CHUNK:2 * CHUNK] * dec, 0.0).astype(BF16)
    t_ms = _inv_unit_lower(n_ms)
    for i, k in enumerate(keys):
        uw = _dot(t_ms[i], jnp.concatenate([_stack_pair(ld(vb_ref, *k)), _stack_pair(ld(kbg_ref, *k))], axis=1))
        us_ref[i] = uw[:, 0:pair_w]
        wq_ref[i] = jnp.concatenate([uw[:, pair_w:2 * pair_w], ld(qd_ref, *k)], axis=0).astype(BF16)
        kdt_ref[i] = (ld(k_ref, *k) * jnp.exp(ld(gl_ref, *k) - ld(gc_ref, *k))).T.astype(BF16)

    chains = [(q_i, p) for q_i in range(n_seq) for p in range(n_pairs)]
    s = [s_ref[q_i * n_pairs + p] for q_i, p in chains]
    for c in range(n_chunks):
        idx = [(q_i * n_chunks + c) * n_pairs + p for q_i, p in chains]
        x = [_dot(wq_ref[i], s[n]) for n, i in enumerate(idx)]
        v_new = [us_ref[i] - x[n][0:CHUNK] for n, i in enumerate(idx)]
        o_c = [x[n][CHUNK:2 * CHUNK] + _dot(qkm_ref[i], _stack_pair(v_new[n])) for n, i in enumerate(idx)]
        for n, ((q_i, p), i) in enumerate(zip(chains, idx)):
            r0 = (q_i * n_chunks + c) * CHUNK
            g_last = jnp.exp(gl_ref[r0:r0 + 1, p * pair_w:(p + 1) * pair_w])
            s[n] = s[n] * g_last + jnp.where(same_head, _dot(kdt_ref[i], v_new[n]), 0.0)
        for q_i in range(n_seq):
            o = jnp.concatenate(o_c[q_i * n_pairs:(q_i + 1) * n_pairs], axis=1)
            ms = _head_sums(o * o, A_HEAD_DIM) * (1.0 / A_HEAD_DIM)
            z = qkvz_ref[q_i, c * CHUNK:(c + 1) * CHUNK, 3 * width:4 * width]
            o_ref[q_i, c * CHUNK:(c + 1) * CHUNK, :] = (
                o * lax.rsqrt(ms + EPS) * nw_ref[...] * _silu(z)).astype(o_ref.dtype)
    for n in range(len(chains)):
        s_ref[n] = s[n]


def _seq_groups(x, n_seq):
    return x.reshape(n_seq, x.shape[0] // n_seq, x.shape[1])


def _gdn(qkvz, ba, conv_w, a_log, dt_bias, norm_w, batch):
    m = qkvz.shape[0]
    width = qkvz.shape[1] // 4
    n_heads = width // A_HEAD_DIM
    rows = GDN_ROWS
    n_seq = GDN_SEQS
    assert batch % n_seq == 0 and (m // batch) % rows == 0
    tblocks = m // batch // rows
    pad = lambda p: jnp.zeros((1, LANES), F32).at[0, n_heads:2 * n_heads].set(p)
    lcat = _chunk_sum_matrix(rows)
    row_spec = lambda w: pl.BlockSpec((n_seq, rows, w), lambda b, i: (0, b * tblocks + i, 0))
    act = lambda: pltpu.VMEM((n_seq * rows, width), F32)
    pair_w = 2 * A_HEAD_DIM
    n_pairs = width // pair_w
    n_keys = n_seq * (rows // CHUNK) * n_pairs
    out = pl.pallas_call(
        _gdn_kernel,
        grid=(batch // n_seq, tblocks),
        in_specs=[row_spec(4 * width), row_spec(LANES), _const_spec(conv_w.shape), _const_spec((1, LANES)),
                  _const_spec((1, LANES)), _const_spec((1, width)), _const_spec(lcat.shape)],
        out_specs=row_spec(width),
        out_shape=jax.ShapeDtypeStruct((n_seq, m // n_seq, width), BF16),
        scratch_shapes=[pltpu.VMEM((n_seq, rows + SUBLANES, 3 * width), F32),
                        pltpu.VMEM((n_seq * n_pairs, pair_w, pair_w), F32)]
                       + [act() for _ in range(8)]
                       + [pltpu.VMEM((n_keys, CHUNK, pair_w), F32), pltpu.VMEM((n_keys, 2 * CHUNK, pair_w), BF16),
                          pltpu.VMEM((n_keys, CHUNK, 2 * CHUNK), BF16), pltpu.VMEM((n_keys, pair_w, CHUNK), BF16)],
        compiler_params=_params(("arbitrary", "arbitrary")),
        name="gdn_mixer",
    )(_seq_groups(qkvz, n_seq), _seq_groups(ba, n_seq), conv_w, pad(a_log), pad(dt_bias),
      jnp.tile(norm_w, n_heads).reshape(1, width), lcat)
    return out.reshape(m, width)


def _rwkv_kernel(pb_ref, mu_ref, w0_ref, w2_ref, a0_ref, a2_ref, g2_ref, kk_ref, ka_ref, rk_ref, lnw_ref,
                 lnb_ref, lcat_ref, o_ref, xp_ref, s_ref, at_ref, rt_ref, bt_ref, kt_ref,
                 v_ref, bv_ref, km_ref, cum_ref, tot_ref, r_ref, gate_ref, lx_ref, tw_ref, mc_ref, bkt_ref,
                 pcol_ref):
    n_seq, rows = pb_ref.shape[0], pb_ref.shape[1]
    width = o_ref.shape[2]
    tb = pl.program_id(1)

    @pl.when(tb == 0)
    def _():
        xp_ref[:, 0:SUBLANES, :] = jnp.zeros((n_seq, SUBLANES, xp_ref.shape[2]), F32)
        s_ref[...] = jnp.zeros(s_ref.shape, F32)

    def gsum(x):
        return _head_sums(x, B_HEAD_DIM)

    for q_i in range(n_seq):
        at = slice(q_i * rows, (q_i + 1) * rows)
        xp = xp_ref.at[q_i]
        p = pb_ref[q_i]
        xp[SUBLANES:SUBLANES + rows, :] = p
        prev = _shifted_rows(xp, 1, rows)
        xp[0:SUBLANES, :] = xp[rows:rows + SUBLANES, :]
        m = p + (prev - p) * mu_ref[...]
        r, kr, vr = m[:, 0:width], m[:, width:2 * width], m[:, 2 * width:3 * width]
        o1 = 3 * width
        o2 = o1 + w2_ref.shape[0]
        o3 = o2 + a2_ref.shape[0]
        w_lo, a_lo, g_lo = m[:, o1:o2], m[:, o2:o3], m[:, o3:]

        lw = -math.exp(-0.5) * _sigmoid(w0_ref[...] + _dot(jnp.tanh(w_lo), w2_ref[...]))
        a_lr = _sigmoid(a0_ref[...] + _dot(a_lo, a2_ref[...]))
        gate_ref[at, :] = _dot(_sigmoid(g_lo), g2_ref[...])

        kk = kr * kk_ref[...]
        k_mod = kr * (1.0 + (a_lr - 1.0) * ka_ref[...])
        kk = kk * lax.rsqrt(gsum(kk * kk) + EPS)
        a_vec = -kk
        b_vec = kk * a_lr

        ct = _dot(lcat_ref[...], lw, pb=2)
        cum, tot = ct[0:rows], ct[rows:2 * rows]
        e_neg = jnp.exp(-cum)
        rt_ref[at, :] = r * jnp.exp(cum)
        at_ref[at, :] = a_vec * jnp.exp(cum - lw)
        bt_ref[at, :] = b_vec * e_neg
        kt_ref[at, :] = k_mod * e_neg
        bv_ref[at, :] = b_vec
        km_ref[at, :] = k_mod
        cum_ref[at, :] = cum
        tot_ref[at, :] = tot
        r_ref[at, :] = r
        v_ref[at, :] = vr

    pair_w = 2 * B_HEAD_DIM
    n_pairs = width // pair_w
    n_chunks = rows // CHUNK
    strict, lower = _pair_masks()
    lower2 = jnp.concatenate([lower, lower], axis=1)
    same_head = (_iota((pair_w, pair_w), 0) < B_HEAD_DIM) == (_iota((pair_w, pair_w), 1) < B_HEAD_DIM)

    def ld(ref, c, pr):
        return ref[c * CHUNK:(c + 1) * CHUNK, pr * pair_w:(pr + 1) * pair_w]

    for q_i in range(n_seq):
        keys = [(q_i * n_chunks + c, pr) for c in range(n_chunks) for pr in range(n_pairs)]
        g = [_dot(jnp.concatenate([ld(at_ref, *k), ld(rt_ref, *k)], axis=0),
                  jnp.concatenate([_stack_pair(ld(bt_ref, *k)), _stack_pair(ld(kt_ref, *k))], axis=0), NT)
             for k in keys]
        t_ms = _inv_unit_lower([jnp.where(strict, gi[0:CHUNK, 0:2 * CHUNK], 0.0) for gi in g])
        w1 = [_dot(jnp.where(strict, g[i][0:CHUNK, 2 * CHUNK:4 * CHUNK], 0.0), _stack_pair(ld(v_ref, *k)))
              for i, k in enumerate(keys)]
        taw = [_dot(t_ms[i], jnp.concatenate([_stack_pair(ld(at_ref, *k)), _stack_pair(w1[i])], axis=1))
               for i, k in enumerate(keys)]
        for i, k in enumerate(keys):
            j = k[0] * n_pairs + k[1]
            lx_ref[j] = jnp.concatenate([taw[i][:, 0:pair_w], ld(rt_ref, *k)], axis=0).astype(BF16)
            tw_ref[j] = taw[i][:, pair_w:2 * pair_w]
            mc_ref[j] = jnp.where(lower2, g[i][CHUNK:2 * CHUNK], 0.0).astype(BF16)
            tot_c = ld(tot_ref, *k)
            e_dec = jnp.exp(tot_c - ld(cum_ref, *k))
            bkt_ref[j] = jnp.concatenate([ld(bv_ref, *k) * e_dec, ld(km_ref, *k) * e_dec], axis=0).T.astype(BF16)
            pcol_ref[j] = jnp.broadcast_to(jnp.exp(tot_c[0:SUBLANES].T[:, 0:1]), (pair_w, pair_w))

    chains = [(q_i, pr) for q_i in range(n_seq) for pr in range(n_pairs)]
    s = [s_ref[q_i * n_pairs + pr] for q_i, pr in chains]
    for c in range(n_chunks):
        gc = [q_i * n_chunks + c for q_i, _ in chains]
        idx = [gc[n] * n_pairs + pr for n, (_, pr) in enumerate(chains)]
        v_c = [ld(v_ref, gc[n], pr) for n, (_, pr) in enumerate(chains)]
        x = [_dot(lx_ref[j], s[n]) for n, j in enumerate(idx)]
        u = [x[n][0:CHUNK] + tw_ref[j] for n, j in enumerate(idx)]
        y_c = [x[n][CHUNK:2 * CHUNK]
               + _dot(mc_ref[j], jnp.concatenate([_stack_pair(u[n]), _stack_pair(v_c[n])], axis=0))
               for n, j in enumerate(idx)]
        for n, j in enumerate(idx):
            upd = _dot(bkt_ref[j], jnp.concatenate([u[n], v_c[n]], axis=0))
            s[n] = s[n] * pcol_ref[j] + jnp.where(same_head, upd, 0.0)
        for q_i in range(n_seq):
            rows_c = slice((q_i * n_chunks + c) * CHUNK, (q_i * n_chunks + c + 1) * CHUNK)
            y = jnp.concatenate(y_c[q_i * n_pairs:(q_i + 1) * n_pairs], axis=1)
            inv_n = 1.0 / B_HEAD_DIM
            mu = gsum(y) * inv_n
            yc = y - mu
            var = gsum(yc * yc) * inv_n
            yn = yc * lax.rsqrt(var + B_LN_EPS) * lnw_ref[...] + lnb_ref[...]
            bonus = gsum(r_ref[rows_c, :] * km_ref[rows_c, :] * rk_ref[...]) * v_ref[rows_c, :]
            o_ref[q_i, c * CHUNK:(c + 1) * CHUNK, :] = ((yn + bonus) * gate_ref[rows_c, :]).astype(o_ref.dtype)
    for n in range(len(chains)):
        s_ref[n] = s[n]


def _rwkv(pb, mu, w0, w2, a0, a2, g2, k_k, k_a, r_k, ln_w, ln_b, batch):
    m, in_w = pb.shape
    width = w0.shape[0]
    rows = MIX_ROWS
    n_seq = MIX_SEQS
    assert batch % n_seq == 0 and (m // batch) % rows == 0
    tblocks = m // batch // rows
    lcat = _chunk_sum_matrix(rows)
    row = lambda p: p.reshape(1, -1).astype(F32)
    row_spec = lambda w: pl.BlockSpec((n_seq, rows, w), lambda b, i: (0, b * tblocks + i, 0))
    vec = _const_spec((1, width))
    act = lambda: pltpu.VMEM((n_seq * rows, width), F32)
    pair_w = 2 * B_HEAD_DIM
    n_pairs = width // pair_w
    n_keys = n_seq * (rows // CHUNK) * n_pairs
    out = pl.pallas_call(
        _rwkv_kernel,
        grid=(batch // n_seq, tblocks),
        in_specs=[row_spec(in_w), _const_spec((1, in_w)), vec, _const_spec(w2.shape), vec,
                  _const_spec(a2.shape), _const_spec(g2.shape), vec, vec, vec, vec, vec,
                  _const_spec(lcat.shape)],
        out_specs=row_spec(width),
        out_shape=jax.ShapeDtypeStruct((n_seq, m // n_seq, width), BF16),
        scratch_shapes=[pltpu.VMEM((n_seq, rows + SUBLANES, in_w), F32),
                        pltpu.VMEM((n_seq * n_pairs, pair_w, pair_w), F32)]
                       + [act() for _ in range(11)]
                       + [pltpu.VMEM((n_keys, 2 * CHUNK, pair_w), BF16), pltpu.VMEM((n_keys, CHUNK, pair_w), F32),
                          pltpu.VMEM((n_keys, CHUNK, 4 * CHUNK), BF16), pltpu.VMEM((n_keys, pair_w, 2 * CHUNK), BF16),
                          pltpu.VMEM((n_keys, pair_w, pair_w), F32)],
        compiler_params=_params(("arbitrary", "arbitrary")),
        name="rwkv_mixer",
    )(_seq_groups(pb, n_seq), row(mu), row(w0), w2, row(a0), a2, g2, row(k_k), row(k_a), row(r_k), row(ln_w),
      row(ln_b), lcat)
    return out.reshape(m, width)


def _rglru_body(gx_ref, cw_ref, cb_ref, wa_ref, ba_ref, wx_ref, bx_ref, lam_ref, o_ref, xp_ref, h_ref):
    rows = gx_ref.shape[0]
    width = o_ref.shape[1]
    xp_ref[SUBLANES:SUBLANES + rows, :] = gx_ref[:, width:2 * width]
    xc = cb_ref[...] + cw_ref[C_CONV - 1:C_CONV, :] * _shifted_rows(xp_ref, 0, rows)
    for back in range(1, C_CONV):
        xc = xc + cw_ref[C_CONV - 1 - back:C_CONV - back, :] * _shifted_rows(xp_ref, back, rows)
    xp_ref[0:SUBLANES, :] = xp_ref[rows:rows + SUBLANES, :]

    gate_r = _sigmoid(_dot(xc, wa_ref[...]) + ba_ref[...])
    gate_i = _sigmoid(_dot(xc, wx_ref[...]) + bx_ref[...])
    log_a = -C_GATE_SCALE * gate_r * _softplus(-lam_ref[...])
    a = jnp.exp(log_a)
    d = jnp.sqrt(jnp.tanh(-log_a) * (a * a + 1.0)) * (gate_i * xc)

    t_idx = _iota((rows, width), 0)
    shift = 1
    while shift < SUBLANES:
        keep = t_idx >= shift
        d = jnp.where(keep, a * pltpu.roll(d, shift, 0) + d, d)
        a = jnp.where(keep, a * pltpu.roll(a, shift, 0), a)
        shift *= 2
    while shift < rows:
        d = jnp.concatenate([d[:shift], a[shift:] * d[:rows - shift] + d[shift:]], axis=0)
        a = jnp.concatenate([a[:shift], a[shift:] * a[:rows - shift]], axis=0)
        shift *= 2
    h = a * h_ref[0:1, :] + d
    h_ref[...] = jnp.broadcast_to(h[rows - 1:rows, :], h_ref.shape)
    o_ref[...] = (h * _gelu_tanh(gx_ref[:, 0:width])).astype(o_ref.dtype)


_LOG_GAMMA = tuple(math.log(1.0 - 2.0 ** (-5.0 - h)) for h in range(D_HEADS))


def _per_head(lane_head, values):
    out = jnp.full(lane_head.shape, values[-1], F32)
    for h in range(len(values) - 2, -1, -1):
        out = jnp.where(lane_head == h, values[h], out)
    return out


def _retention_body(qk_ref, vg_ref, cos_ref, sin_ref, gw_ref, gb_ref, o_ref, s_ref, dm_ref):
    rows = qk_ref.shape[0]
    kw = qk_ref.shape[1] // 2
    vw = o_ref.shape[1]

    def rotary(x):
        even = _mod(_iota((rows, LANES), 1), 2) == 0
        parts = []
        for c in range(kw // LANES):
            xc = x[:, c * LANES:(c + 1) * LANES]
            parts.append(jnp.where(even, pltpu.roll(xc, LANES - 1, 1), pltpu.roll(xc, 1, 1)))
        return x * cos_ref[...] + jnp.concatenate(parts, axis=1) * sin_ref[...]

    q = rotary(qk_ref[:, 0:kw])
    k = rotary(qk_ref[:, kw:2 * kw]) * (D_KEY_DIM ** -0.5)
    v = vg_ref[:, 0:vw]
    k_head = _div(_iota((1, kw), 1), D_KEY_DIM)
    lg_k = _per_head(k_head, _LOG_GAMMA)
    lg_v = _per_head(_div(_iota((1, vw), 1), D_VAL_DIM), _LOG_GAMMA)
    t_idx = _iota((rows, 1), 0).astype(F32)

    o_cross = _dot(q * jnp.exp((t_idx + 1.0) * lg_k), s_ref[...])
    inner = []
    for h in range(D_HEADS):
        scores = _dot(jnp.where(k_head == h, q, 0.0), k, NT) * dm_ref[h]
        inner.append(_dot(scores, v[:, h * D_VAL_DIM:(h + 1) * D_VAL_DIM]))
    o = o_cross + jnp.concatenate(inner, axis=1)

    k_dec = k * jnp.exp((rows - 1.0 - t_idx) * lg_k)
    same_head = _div(_iota((kw, vw), 0), D_KEY_DIM) == _div(_iota((kw, vw), 1), D_VAL_DIM)
    s_ref[...] = s_ref[...] * jnp.exp(float(rows) * lg_v) + jnp.where(same_head, _dot(k_dec, v, TN), 0.0)

    inv_n = 1.0 / D_VAL_DIM
    mu = _head_sums(o, D_VAL_DIM) * inv_n
    oc = o - mu
    var = _head_sums(oc * oc, D_VAL_DIM) * inv_n
    on = oc * lax.rsqrt(var + EPS) * gw_ref[...] + gb_ref[...]
    o_ref[...] = (_silu(vg_ref[:, vw:2 * vw]) * on).astype(o_ref.dtype)


def _mixer_cd_kernel(gx_ref, cw_ref, cb_ref, wa_ref, ba_ref, wx_ref, bx_ref, lam_ref, qk_ref, vg_ref, cos_ref,
                     sin_ref, gw_ref, gb_ref, oc_ref, od_ref, xp_ref, h_ref, s_ref, dm_ref):
    rows = qk_ref.shape[0]

    @pl.when((pl.program_id(0) == 0) & (pl.program_id(1) == 0))
    def _():
        rel = (_iota((rows, rows), 0) - _iota((rows, rows), 1)).astype(F32)
        for h in range(D_HEADS):
            dm_ref[h] = jnp.where(rel >= 0, jnp.exp(jnp.maximum(rel, 0.0) * _LOG_GAMMA[h]), 0.0)

    @pl.when(pl.program_id(1) == 0)
    def _():
        xp_ref[0:SUBLANES, :] = jnp.zeros((SUBLANES, xp_ref.shape[1]), F32)
        h_ref[...] = jnp.zeros(h_ref.shape, F32)
        s_ref[...] = jnp.zeros(s_ref.shape, F32)

    _retention_body(qk_ref, vg_ref, cos_ref, sin_ref, gw_ref, gb_ref, od_ref, s_ref, dm_ref)
    _rglru_body(gx_ref, cw_ref, cb_ref, wa_ref, ba_ref, wx_ref, bx_ref, lam_ref, oc_ref, xp_ref, h_ref)


def _mixer_cd(gx, qk, vg, conv_w, conv_b, wa, ba, wx, bx, lam, gn_w, gn_b, batch):
    m = qk.shape[0]
    cw = gx.shape[1] // 2
    kw = qk.shape[1] // 2
    vw = vg.shape[1] // 2
    rows = MIX_ROWS
    seq = m // batch
    tblocks = seq // rows
    inv = 1.0 / (ROPE_BASE ** jnp.linspace(0.0, 1.0, D_KEY_DIM // 2, dtype=F32))
    ang = jnp.arange(seq).astype(F32)[:, None] * inv[None, :]
    cos = jnp.tile(jnp.repeat(jnp.cos(ang), 2, axis=1), (1, kw // D_KEY_DIM))
    sin = jnp.tile(jnp.stack([-jnp.sin(ang), jnp.sin(ang)], axis=-1).reshape(seq, D_KEY_DIM),
                   (1, kw // D_KEY_DIM))
    row = lambda p: p.reshape(1, -1).astype(F32)
    blockdiag = lambda w: jax.scipy.linalg.block_diag(*[w[i] for i in range(w.shape[0])]).astype(BF16)
    row_spec = lambda w: pl.BlockSpec((rows, w), lambda b, i: (b * tblocks + i, 0))
    tab_spec = pl.BlockSpec((rows, kw), lambda b, i: (i, 0))
    cvec, cmat, vvec = _const_spec((1, cw)), _const_spec((cw, cw)), _const_spec((1, vw))
    return pl.pallas_call(
        _mixer_cd_kernel,
        grid=(batch, tblocks),
        in_specs=[row_spec(2 * cw), _const_spec(conv_w.shape), cvec, cmat, cvec, cmat, cvec, cvec,
                  row_spec(2 * kw), row_spec(2 * vw), tab_spec, tab_spec, vvec, vvec],
        out_specs=[row_spec(cw), row_spec(vw)],
        out_shape=[jax.ShapeDtypeStruct((m, cw), BF16), jax.ShapeDtypeStruct((m, vw), BF16)],
        scratch_shapes=[pltpu.VMEM((rows + SUBLANES, cw), F32), pltpu.VMEM((SUBLANES, cw), F32),
                        pltpu.VMEM((kw, vw), F32), pltpu.VMEM((D_HEADS, rows, rows), F32)],
        compiler_params=_params(("arbitrary", "arbitrary")),
        name="rglru_retention_mixer",
    )(gx, conv_w, row(conv_b), blockdiag(wa), row(ba), blockdiag(wx), row(bx), row(lam),
      qk, vg, cos, sin, row(gn_w), row(gn_b))


def _ffn_kernel(x_ref, xh_ref, oa_ref, oah_ref, ob_ref, obh_ref, wo_ref, g_ref, wup_ref, wc_ref, wd_ref,
                *rest, seq_blocks, final):
    if final:
        gf_ref, y_ref, hc_ref, u_ref = rest
    else:
        y_ref, hc_ref, u_ref = rest
    rows = x_ref.shape[0]
    half = oa_ref.shape[1]
    ffn = wd_ref.shape[0]

    def mixed(x, oa, ob):
        return (x + jnp.dot(oa, wo_ref[0:half, :], preferred_element_type=F32)
                + jnp.dot(ob, wo_ref[half:2 * half, :], preferred_element_type=F32))

    x1 = mixed(x_ref[...], oa_ref[...], ob_ref[...])
    x1h = mixed(xh_ref[...], oah_ref[...], obh_ref[...])
    starts_seq = (pl.program_id(0) % seq_blocks) == 0
    hc_ref[0:BF16_ROWS, :] = (_rms(x1h, g_ref[...]) * jnp.where(starts_seq, 0.0, 1.0)).astype(BF16)
    hc_ref[BF16_ROWS:BF16_ROWS + rows, :] = _rms(x1, g_ref[...]).astype(BF16)

    u_ref[...] = jnp.dot(hc_ref[...], wup_ref[:, 0:ffn], preferred_element_type=F32)
    v = jnp.dot(hc_ref[BF16_ROWS:BF16_ROWS + rows, :], wup_ref[:, ffn:2 * ffn], preferred_element_type=F32)
    uc = wc_ref[FFN_CONV - 1:FFN_CONV, :] * u_ref[pl.ds(BF16_ROWS, rows), :]
    for back in range(1, FFN_CONV):
        uc = uc + wc_ref[FFN_CONV - 1 - back:FFN_CONV - back, :] * u_ref[pl.ds(BF16_ROWS - back, rows), :]
    gl = (_silu(uc) * v).astype(BF16)
    y = x1 + jnp.dot(gl, wd_ref[...], preferred_element_type=F32)
    if final:
        y = _rms(y, gf_ref[...])
    y_ref[...] = y


def _ffn(x, oa, ob, w_out, norm_w, w_up, w_conv, w_down, seq, final_norm=None):
    m, d = x.shape
    half = oa.shape[1]
    ffn = w_down.shape[0]
    tm = FFN_ROWS
    halo = BF16_ROWS
    final = final_norm is not None
    row_spec = lambda w: pl.BlockSpec((tm, w), lambda i: (i, 0))
    halo_spec = lambda w: pl.BlockSpec((halo, w), lambda i: (jnp.maximum(i * (tm // halo) - 1, 0), 0))
    resident = lambda w: pl.BlockSpec(w.shape, lambda i: (0, 0), pipeline_mode=pl.Buffered(1))
    row = lambda p: p.reshape(1, -1).astype(F32)
    in_specs = [row_spec(d), halo_spec(d), row_spec(half), halo_spec(half), row_spec(half), halo_spec(half),
                resident(w_out), _const_spec((1, d)), resident(w_up), _const_spec(w_conv.shape), resident(w_down)]
    args = [x, x, oa, oa, ob, ob, w_out, row(norm_w), w_up, w_conv, w_down]
    if final:
        in_specs.append(_const_spec((1, d)))
        args.append(row(final_norm))
    return pl.pallas_call(
        functools.partial(_ffn_kernel, seq_blocks=seq // tm, final=final),
        grid=(m // tm,),
        in_specs=in_specs,
        out_specs=row_spec(d),
        out_shape=jax.ShapeDtypeStruct((m, d), F32),
        scratch_shapes=[pltpu.VMEM((tm + halo, d), BF16), pltpu.VMEM((tm + halo, ffn), F32)],
        compiler_params=_params(("parallel",)),
        name="outproj_ffn_final" if final else "outproj_ffn",
    )(*args)


def kernel(x, l0_norm1, l0_w_in, l0_a_conv, l0_a_A_log, l0_a_dt_bias, l0_a_norm, l0_b_mu, l0_b_w0, l0_b_w2, l0_b_a0, l0_b_a2, l0_b_g2, l0_b_k_k, l0_b_k_a, l0_b_r_k, l0_b_ln_w, l0_b_ln_b, l0_w_out, l0_norm2, l0_ffn_up, l0_ffn_conv, l0_ffn_down, l1_norm1, l1_w_in, l1_c_conv_w, l1_c_conv_b, l1_c_wa, l1_c_ba, l1_c_wx, l1_c_bx, l1_c_lambda, l1_d_gn_w, l1_d_gn_b, l1_w_out, l1_norm2, l1_ffn_up, l1_ffn_conv, l1_ffn_down, final_norm):
    batch, seq, d = x.shape
    xf = x.reshape(batch * seq, d)
    bf = lambda w: w.astype(BF16)

    a_w = l0_a_conv.shape[1] // 3
    a_heads = l0_a_A_log.shape[0]
    a_in = 4 * a_w + 2 * a_heads
    w_gates = jnp.concatenate([bf(l0_w_in[:, 4 * a_w:a_in]), jnp.zeros((d, LANES - 2 * a_heads), BF16)], axis=1)
    b_in = l0_w_in.shape[1] - a_in
    qkvz, ba, pb = _norm_proj(xf, l0_norm1, [bf(l0_w_in[:, 0:4 * a_w]), w_gates, bf(l0_w_in[:, a_in:])],
                              ((4 * a_w,), (LANES,), (b_in,)), "norm_proj0")
    o_a = _gdn(qkvz, ba, l0_a_conv, l0_a_A_log, l0_a_dt_bias, l0_a_norm, batch)
    o_bb = _rwkv(pb, l0_b_mu, l0_b_w0, bf(l0_b_w2), l0_b_a0, bf(l0_b_a2), bf(l0_b_g2),
                 l0_b_k_k, l0_b_k_a, l0_b_r_k, l0_b_ln_w, l0_b_ln_b, batch)
    x1 = _ffn(xf, o_a, o_bb, bf(l0_w_out), l0_norm2, bf(l0_ffn_up), l0_ffn_conv, bf(l0_ffn_down), seq)

    c_w = l1_c_lambda.shape[0]
    d_w = l1_d_gn_w.shape[0]
    qk_w = l1_w_in.shape[1] - 2 * c_w - 2 * d_w
    gx, qk, vg = _norm_proj(x1, l1_norm1, [bf(l1_w_in)], ((2 * c_w, qk_w, 2 * d_w),), "norm_proj1")
    o_c, o_d = _mixer_cd(gx, qk, vg, l1_c_conv_w, l1_c_conv_b, l1_c_wa, l1_c_ba, l1_c_wx, l1_c_bx, l1_c_lambda,
                         l1_d_gn_w, l1_d_gn_b, batch)
    y = _ffn(x1, o_c, o_d, bf(l1_w_out), l1_norm2, bf(l1_ffn_up), l1_ffn_conv, bf(l1_ffn_down), seq,
             final_norm=final_norm)
    return y.reshape(batch, seq, d)
```

```python
import functools
import math

import numpy as np
import jax
import jax.numpy as jnp
from jax import lax
from jax.experimental import pallas as pl
from jax.experimental.pallas import tpu as pltpu

F32 = jnp.float32
BF16 = jnp.bfloat16

EPS = 1e-6
CHUNK = 64
A_HEAD_DIM = 128
A_CONV = 4
B_HEAD_DIM = 64
B_LN_EPS = 64e-5
C_GATE_SCALE = 8.0
C_CONV = 4
D_KEY_DIM = 64
D_VAL_DIM = 128
D_HEADS = 4
ROPE_BASE = 10000.0
FFN_CONV = 3

LANES = 128
SUBLANES = 8
BF16_ROWS = 16
VMEM_LIMIT = 56 * 1024 * 1024

MIX_ROWS = 256
MIX_SEQS = 2
GDN_ROWS = 128
GDN_SEQS = 4
PROJ_ROWS = 512
FFN_ROWS = 512

NN = (((1,), (0,)), ((), ()))
NT = (((1,), (1,)), ((), ()))
TN = (((0,), (0,)), ((), ()))


def _pieces(x, n):
    if x.dtype == BF16:
        return [x]
    out, r = [], x
    for i in range(n):
        p = r.astype(BF16)
        out.append(p)
        if i + 1 < n:
            r = r - p.astype(F32)
    return out


def _dot(a, b, dims=NN, pa=1, pb=1):
    ap, bp = _pieces(a, pa), _pieces(b, pb)
    order = max(len(ap), len(bp))
    acc = None
    for i in reversed(range(len(ap))):
        for j in reversed(range(len(bp))):
            if i + j < order:
                t = lax.dot_general(ap[i], bp[j], dims, preferred_element_type=F32)
                acc = t if acc is None else acc + t
    return acc


def _sigmoid(x):
    return 0.5 * jnp.tanh(0.5 * x) + 0.5


def _silu(x):
    return x * _sigmoid(x)


def _softplus(x):
    return jnp.maximum(x, 0.0) + jnp.log1p(jnp.exp(-jnp.abs(x)))


def _gelu_tanh(x):
    return 0.5 * x * (1.0 + jnp.tanh(math.sqrt(2.0 / math.pi) * (x + 0.044715 * (x * x * x))))


def _rms(x, g):
    return x * lax.rsqrt(jnp.mean(x * x, axis=-1, keepdims=True) + EPS) * g


def _iota(shape, dim):
    return lax.broadcasted_iota(jnp.int32, shape, dim)


def _div(i, n):
    assert n & (n - 1) == 0
    return i >> (n.bit_length() - 1)


def _mod(i, n):
    assert n & (n - 1) == 0
    return i & (n - 1)


def _pair_masks():
    t, s = _iota((CHUNK, 2 * CHUNK), 0), _mod(_iota((CHUNK, 2 * CHUNK), 1), CHUNK)
    return s < t, s <= t


def _stack_heads(x, n):
    head = _div(_iota(x.shape, 1), x.shape[1] // n)
    return jnp.concatenate([jnp.where(head == h, x, 0.0) for h in range(n)], axis=0)


def _stack_pair(x):
    return _stack_heads(x, 2)


def _inv_unit_lower(n_mats):
    shape = n_mats[0].shape
    heads = shape[1] // CHUNK
    stack = functools.partial(_stack_heads, n=heads)
    eye = (_mod(_iota(shape, 1), CHUNK) == _iota(shape, 0)).astype(F32)
    ts = [eye + n for n in n_mats]
    ps = [_dot(n, stack(n)) for n in n_mats]
    levels = int(math.log2(CHUNK)) - 1
    for level in range(levels):
        last = level == levels - 1
        lhs = ts if last else [jnp.concatenate([t, p], axis=0) for t, p in zip(ts, ps)]
        prod = [_dot(l, stack(p)) for l, p in zip(lhs, ps)]
        ts = [t + pr[0:CHUNK] for t, pr in zip(ts, prod)]
        if not last:
            ps = [pr[CHUNK:2 * CHUNK] for pr in prod]
    return ts


def _head_sums(x, dim):
    assert dim in (LANES, LANES // 2)
    low = _iota((x.shape[0], LANES), 1) < dim
    parts = []
    for c in range(x.shape[1] // LANES):
        xc = x[:, c * LANES:(c + 1) * LANES]
        total = jnp.sum(xc, axis=-1, keepdims=True)
        if dim == LANES:
            parts.append(jnp.broadcast_to(total, xc.shape))
        else:
            first = jnp.sum(jnp.where(low, xc, 0.0), axis=-1, keepdims=True)
            parts.append(jnp.where(low, first, total - first))
    return jnp.concatenate(parts, axis=1)


def _head_expand(cols, first, heads, dim):
    return jnp.concatenate([jnp.broadcast_to(cols[:, first + h:first + h + 1], (cols.shape[0], dim))
                            for h in range(heads)], axis=1)


def _shifted_rows(xp_ref, back, rows):
    if back == 0:
        return xp_ref[SUBLANES:SUBLANES + rows, :]
    return pltpu.roll(xp_ref[...], back, 0)[SUBLANES:SUBLANES + rows, :]


def _params(sem, fusible=None):
    return pltpu.CompilerParams(dimension_semantics=sem, vmem_limit_bytes=VMEM_LIMIT, allow_input_fusion=fusible)


def _const_spec(shape):
    return pl.BlockSpec(shape, lambda *_: (0,) * len(shape))


def _chunk_sum_matrix(rows):
    t = np.arange(rows)
    same = (t[:, None] // CHUNK) == (t[None, :] // CHUNK)
    return jnp.asarray(np.concatenate([same & (t[None, :] <= t[:, None]), same], axis=0), BF16)


def _norm_proj_kernel(x_ref, g_ref, *refs, splits):
    w_refs, o_refs = refs[:len(splits)], refs[len(splits):]
    h = _rms(x_ref[...], g_ref[...]).astype(BF16)
    outs = iter(o_refs)
    for w_ref, cols in zip(w_refs, splits):
        off = 0
        for n in cols:
            next(outs)[...] = jnp.dot(h, w_ref[:, off:off + n], preferred_element_type=F32)
            off += n


def _norm_proj(x, g, ws, splits, name):
    m, d = x.shape
    tm = PROJ_ROWS
    widths = [n for cols in splits for n in cols]
    return pl.pallas_call(
        functools.partial(_norm_proj_kernel, splits=splits),
        grid=(m // tm,),
        in_specs=[pl.BlockSpec((tm, d), lambda i: (i, 0)), _const_spec((1, d))] + [_const_spec(w.shape) for w in ws],
        out_specs=[pl.BlockSpec((tm, n), lambda i: (i, 0)) for n in widths],
        out_shape=[jax.ShapeDtypeStruct((m, n), F32) for n in widths],
        compiler_params=_params(("parallel",), [False, False] + [True] * len(ws)),
        name=name,
    )(x, g.reshape(1, d), *ws)


def _gdn_kernel(qkvz_ref, ba_ref, cw_ref, alog_ref, dtb_ref, nw_ref, lcat_ref,
                o_ref, xp_ref, s_ref, qs_ref, k_ref, kb_ref, vb_ref, kbg_ref, qd_ref, gc_ref,
                gl_ref, us_ref, wq_ref, qkm_ref, kdt_ref):
    n_seq, rows = qkvz_ref.shape[0], qkvz_ref.shape[1]
    width = o_ref.shape[2]
    n_heads = width // A_HEAD_DIM
    tb = pl.program_id(1)

    @pl.when(tb == 0)
    def _():
        xp_ref[:, 0:SUBLANES, :] = jnp.zeros((n_seq, SUBLANES, xp_ref.shape[2]), F32)
        s_ref[...] = jnp.zeros(s_ref.shape, F32)

    def l2n(x):
        return x * lax.rsqrt(_head_sums(x * x, A_HEAD_DIM) + EPS)

    for q_i in range(n_seq):
        at = slice(q_i * rows, (q_i + 1) * rows)
        xp = xp_ref.at[q_i]
        xp[SUBLANES:SUBLANES + rows, :] = qkvz_ref[q_i, :, 0:3 * width]
        acc = cw_ref[A_CONV - 1:A_CONV, :] * _shifted_rows(xp, 0, rows)
        for back in range(1, A_CONV):
            acc = acc + cw_ref[A_CONV - 1 - back:A_CONV - back, :] * _shifted_rows(xp, back, rows)
        xp[0:SUBLANES, :] = xp[rows:rows + SUBLANES, :]
        act = _silu(acc)
        q, k, v = act[:, 0:width], act[:, width:2 * width], act[:, 2 * width:3 * width]

        ba = ba_ref[q_i]
        bg = jnp.where(_iota(ba.shape, 1) < n_heads, _sigmoid(ba),
                       -jnp.exp(alog_ref[...]) * _softplus(ba + dtb_ref[...]))
        ct = _dot(lcat_ref[...], bg, pb=3)
        beta_f = _head_expand(bg, 0, n_heads, A_HEAD_DIM)
        gc_f = _head_expand(ct[0:rows], n_heads, n_heads, A_HEAD_DIM)
        gl_f = _head_expand(ct[rows:2 * rows], n_heads, n_heads, A_HEAD_DIM)

        kn = l2n(k)
        qs = l2n(q) * (A_HEAD_DIM ** -0.5)
        kb = kn * beta_f
        e_gc = jnp.exp(gc_f)
        qs_ref[at, :] = qs
        k_ref[at, :] = kn
        kb_ref[at, :] = kb
        vb_ref[at, :] = v * beta_f
        kbg_ref[at, :] = kb * e_gc
        qd_ref[at, :] = qs * e_gc
        gc_ref[at, :] = gc_f
        gl_ref[at, :] = gl_f

    pair_w = 2 * A_HEAD_DIM
    assert A_HEAD_DIM == 2 * CHUNK
    n_pairs = width // pair_w
    n_chunks = rows // CHUNK
    strict, lower = _pair_masks()
    first_head = _iota((CHUNK, 2 * CHUNK), 1) < CHUNK
    same_head = (_iota((pair_w, pair_w), 0) < A_HEAD_DIM) == (_iota((pair_w, pair_w), 1) < A_HEAD_DIM)

    def ld(ref, c, p):
        return ref[c * CHUNK:(c + 1) * CHUNK, p * pair_w:(p + 1) * pair_w]

    keys = [(c, p) for c in range(n_seq * n_chunks) for p in range(n_pairs)]
    g = [_dot(jnp.concatenate([ld(kb_ref, *k), ld(qs_ref, *k)], axis=0), _stack_pair(ld(k_ref, *k)), NT)
         for k in keys]
    n_ms = []
    for i, k in enumerate(keys):
        gcp = ld(gc_ref, *k)
        g_t = jnp.where(first_head, gcp[:, 0:A_HEAD_DIM], gcp[:, A_HEAD_DIM:pair_w])
        g_s = jnp.concatenate([gcp[:, 0:A_HEAD_DIM], gcp[:, A_HEAD_DIM:pair_w]], axis=0).T[0:CHUNK]
        dec = jnp.exp(jnp.where(lower, g_t - g_s, 0.0))
        n_ms.append(jnp.where(strict, -(g[i][0:CHUNK] * dec), 0.0))
        qkm_ref[i] = jnp.where(lower, g[i][CHUNK:2 * CHUNK] * dec, 0.0).astype(BF16)
    t_ms = _inv_unit_lower(n_ms)
    for i, k in enumerate(keys):
        uw = _dot(t_ms[i], jnp.concatenate([_stack_pair(ld(vb_ref, *k)), _stack_pair(ld(kbg_ref, *k))], axis=1))
        us_ref[i] = uw[:, 0:pair_w]
        wq_ref[i] = jnp.concatenate([uw[:, pair_w:2 * pair_w], ld(qd_ref, *k)], axis=0).astype(BF16)
        kdt_ref[i] = (ld(k_ref, *k) * jnp.exp(ld(gl_ref, *k) - ld(gc_ref, *k))).T.astype(BF16)

    chains = [(q_i, p) for q_i in range(n_seq) for p in range(n_pairs)]
    s = [s_ref[q_i * n_pairs + p] for q_i, p in chains]
    for c in range(n_chunks):
        idx = [(q_i * n_chunks + c) * n_pairs + p for q_i, p in chains]
        x = [_dot(wq_ref[i], s[n]) for n, i in enumerate(idx)]
        v_new = [us_ref[i] - x[n][0:CHUNK] for n, i in enumerate(idx)]
        o_c = [x[n][CHUNK:2 * CHUNK] + _dot(qkm_ref[i], _stack_pair(v_new[n])) for n, i in enumerate(idx)]
        for n, ((q_i, p), i) in enumerate(zip(chains, idx)):
            r0 = (q_i * n_chunks + c) * CHUNK
            g_last = jnp.exp(gl_ref[r0:r0 + 1, p * pair_w:(p + 1) * pair_w])
            s[n] = s[n] * g_last + jnp.where(same_head, _dot(kdt_ref[i], v_new[n]), 0.0)
        for q_i in range(n_seq):
            o = jnp.concatenate(o_c[q_i * n_pairs:(q_i + 1) * n_pairs], axis=1)
            ms = _head_sums(o * o, A_HEAD_DIM) * (1.0 / A_HEAD_DIM)
            z = qkvz_ref[q_i, c * CHUNK:(c + 1) * CHUNK, 3 * width:4 * width]
            o_ref[q_i, c * CHUNK:(c + 1) * CHUNK, :] = (
                o * lax.rsqrt(ms + EPS) * nw_ref[...] * _silu(z)).astype(o_ref.dtype)
    for n in range(len(chains)):
        s_ref[n] = s[n]


def _seq_groups(x, n_seq):
    return x.reshape(n_seq, x.shape[0] // n_seq, x.shape[1])


def _gdn(qkvz, ba, conv_w, a_log, dt_bias, norm_w, batch):
    m = qkvz.shape[0]
    width = qkvz.shape[1] // 4
    n_heads = width // A_HEAD_DIM
    rows = GDN_ROWS
    n_seq = GDN_SEQS
    assert batch % n_seq == 0 and (m // batch) % rows == 0
    tblocks = m // batch // rows
    pad = lambda p: jnp.zeros((1, LANES), F32).at[0, n_heads:2 * n_heads].set(p)
    lcat = _chunk_sum_matrix(rows)
    row_spec = lambda w: pl.BlockSpec((n_seq, rows, w), lambda b, i: (0, b * tblocks + i, 0))
    act = lambda: pltpu.VMEM((n_seq * rows, width), F32)
    pair_w = 2 * A_HEAD_DIM
    n_pairs = width // pair_w
    n_keys = n_seq * (rows // CHUNK) * n_pairs
    out = pl.pallas_call(
        _gdn_kernel,
        grid=(batch // n_seq, tblocks),
        in_specs=[row_spec(4 * width), row_spec(LANES), _const_spec(conv_w.shape), _const_spec((1, LANES)),
                  _const_spec((1, LANES)), _const_spec((1, width)), _const_spec(lcat.shape)],
        out_specs=row_spec(width),
        out_shape=jax.ShapeDtypeStruct((n_seq, m // n_seq, width), BF16),
        scratch_shapes=[pltpu.VMEM((n_seq, rows + SUBLANES, 3 * width), F32),
                        pltpu.VMEM((n_seq * n_pairs, pair_w, pair_w), F32)]
                       + [act() for _ in range(8)]
                       + [pltpu.VMEM((n_keys, CHUNK, pair_w), F32), pltpu.VMEM((n_keys, 2 * CHUNK, pair_w), BF16),
                          pltpu.VMEM((n_keys, CHUNK, 2 * CHUNK), BF16), pltpu.VMEM((n_keys, pair_w, CHUNK), BF16)],
        compiler_params=_params(("arbitrary", "arbitrary")),
        name="gdn_mixer",
    )(_seq_groups(qkvz, n_seq), _seq_groups(ba, n_seq), conv_w, pad(a_log), pad(dt_bias),
      jnp.tile(norm_w, n_heads).reshape(1, width), lcat)
    return out.reshape(m, width)


def _rwkv_kernel(pb_ref, mu_ref, w0_ref, w2_ref, a0_ref, a2_ref, g2_ref, kk_ref, ka_ref, rk_ref, lnw_ref,
                 lnb_ref, lcat_ref, o_ref, xp_ref, s_ref, at_ref, rt_ref, bt_ref, kt_ref,
                 v_ref, bv_ref, km_ref, cum_ref, tot_ref, r_ref, gate_ref, lx_ref, tw_ref, mc_ref, bkt_ref,
                 pcol_ref):
    n_seq, rows = pb_ref.shape[0], pb_ref.shape[1]
    width = o_ref.shape[2]
    tb = pl.program_id(1)

    @pl.when(tb == 0)
    def _():
        xp_ref[:, 0:SUBLANES, :] = jnp.zeros((n_seq, SUBLANES, xp_ref.shape[2]), F32)
        s_ref[...] = jnp.zeros(s_ref.shape, F32)

    def gsum(x):
        return _head_sums(x, B_HEAD_DIM)

    for q_i in range(n_seq):
        at = slice(q_i * rows, (q_i + 1) * rows)
        xp = xp_ref.at[q_i]
        p = pb_ref[q_i]
        xp[SUBLANES:SUBLANES + rows, :] = p
        prev = _shifted_rows(xp, 1, rows)
        xp[0:SUBLANES, :] = xp[rows:rows + SUBLANES, :]
        m = p + (prev - p) * mu_ref[...]
        r, kr, vr = m[:, 0:width], m[:, width:2 * width], m[:, 2 * width:3 * width]
        o1 = 3 * width
        o2 = o1 + w2_ref.shape[0]
        o3 = o2 + a2_ref.shape[0]
        w_lo, a_lo, g_lo = m[:, o1:o2], m[:, o2:o3], m[:, o3:]

        lw = -math.exp(-0.5) * _sigmoid(w0_ref[...] + _dot(jnp.tanh(w_lo), w2_ref[...]))
        a_lr = _sigmoid(a0_ref[...] + _dot(a_lo, a2_ref[...]))
        gate_ref[at, :] = _dot(_sigmoid(g_lo), g2_ref[...])

        kk = kr * kk_ref[...]
        k_mod = kr * (1.0 + (a_lr - 1.0) * ka_ref[...])
        kk = kk * lax.rsqrt(gsum(kk * kk) + EPS)
        a_vec = -kk
        b_vec = kk * a_lr

        ct = _dot(lcat_ref[...], lw, pb=2)
        cum, tot = ct[0:rows], ct[rows:2 * rows]
        e_neg = jnp.exp(-cum)
        rt_ref[at, :] = r * jnp.exp(cum)
        at_ref[at, :] = a_vec * jnp.exp(cum - lw)
        bt_ref[at, :] = b_vec * e_neg
        kt_ref[at, :] = k_mod * e_neg
        bv_ref[at, :] = b_vec
        km_ref[at, :] = k_mod
        cum_ref[at, :] = cum
        tot_ref[at, :] = tot
        r_ref[at, :] = r
        v_ref[at, :] = vr

    pair_w = 2 * B_HEAD_DIM
    n_pairs = width // pair_w
    n_chunks = rows // CHUNK
    strict, lower = _pair_masks()
    lower2 = jnp.concatenate([lower, lower], axis=1)
    same_head = (_iota((pair_w, pair_w), 0) < B_HEAD_DIM) == (_iota((pair_w, pair_w), 1) < B_HEAD_DIM)

    def ld(ref, c, pr):
        return ref[c * CHUNK:(c + 1) * CHUNK, pr * pair_w:(pr + 1) * pair_w]

    for q_i in range(n_seq):
        keys = [(q_i * n_chunks + c, pr) for c in range(n_chunks) for pr in range(n_pairs)]
        g = [_dot(jnp.concatenate([ld(at_ref, *k), ld(rt_ref, *k)], axis=0),
                  jnp.concatenate([_stack_pair(ld(bt_ref, *k)), _stack_pair(ld(kt_ref, *k))], axis=0), NT)
             for k in keys]
        t_ms = _inv_unit_lower([jnp.where(strict, gi[0:CHUNK, 0:2 * CHUNK], 0.0) for gi in g])
        w1 = [_dot(jnp.where(strict, g[i][0:CHUNK, 2 * CHUNK:4 * CHUNK], 0.0), _stack_pair(ld(v_ref, *k)))
              for i, k in enumerate(keys)]
        taw = [_dot(t_ms[i], jnp.concatenate([_stack_pair(ld(at_ref, *k)), _stack_pair(w1[i])], axis=1))
               for i, k in enumerate(keys)]
        for i, k in enumerate(keys):
            j = k[0] * n_pairs + k[1]
            lx_ref[j] = jnp.concatenate([taw[i][:, 0:pair_w], ld(rt_ref, *k)], axis=0).astype(BF16)
            tw_ref[j] = taw[i][:, pair_w:2 * pair_w]
            mc_ref[j] = jnp.where(lower2, g[i][CHUNK:2 * CHUNK], 0.0).astype(BF16)
            tot_c = ld(tot_ref, *k)
            e_dec = jnp.exp(tot_c - ld(cum_ref, *k))
            bkt_ref[j] = jnp.concatenate([ld(bv_ref, *k) * e_dec, ld(km_ref, *k) * e_dec], axis=0).T.astype(BF16)
            pcol_ref[j] = jnp.broadcast_to(jnp.exp(tot_c[0:SUBLANES].T[:, 0:1]), (pair_w, pair_w))

    chains = [(q_i, pr) for q_i in range(n_seq) for pr in range(n_pairs)]
    s = [s_ref[q_i * n_pairs + pr] for q_i, pr in chains]
    for c in range(n_chunks):
        gc = [q_i * n_chunks + c for q_i, _ in chains]
        idx = [gc[n] * n_pairs + pr for n, (_, pr) in enumerate(chains)]
        v_c = [ld(v_ref, gc[n], pr) for n, (_, pr) in enumerate(chains)]
        x = [_dot(lx_ref[j], s[n]) for n, j in enumerate(idx)]
        u = [x[n][0:CHUNK] + tw_ref[j] for n, j in enumerate(idx)]
        y_c = [x[n][CHUNK:2 * CHUNK]
               + _dot(mc_ref[j], jnp.concatenate([_stack_pair(u[n]), _stack_pair(v_c[n])], axis=0))
               for n, j in enumerate(idx)]
        for n, j in enumerate(idx):
            upd = _dot(bkt_ref[j], jnp.concatenate([u[n], v_c[n]], axis=0))
            s[n] = s[n] * pcol_ref[j] + jnp.where(same_head, upd, 0.0)
        for q_i in range(n_seq):
            rows_c = slice((q_i * n_chunks + c) * CHUNK, (q_i * n_chunks + c + 1) * CHUNK)
            y = jnp.concatenate(y_c[q_i * n_pairs:(q_i + 1) * n_pairs], axis=1)
            inv_n = 1.0 / B_HEAD_DIM
            mu = gsum(y) * inv_n
            yc = y - mu
            var = gsum(yc * yc) * inv_n
            yn = yc * lax.rsqrt(var + B_LN_EPS) * lnw_ref[...] + lnb_ref[...]
            bonus = gsum(r_ref[rows_c, :] * km_ref[rows_c, :] * rk_ref[...]) * v_ref[rows_c, :]
            o_ref[q_i, c * CHUNK:(c + 1) * CHUNK, :] = ((yn + bonus) * gate_ref[rows_c, :]).astype(o_ref.dtype)
    for n in range(len(chains)):
        s_ref[n] = s[n]


def _rwkv(pb, mu, w0, w2, a0, a2, g2, k_k, k_a, r_k, ln_w, ln_b, batch):
    m, in_w = pb.shape
    width = w0.shape[0]
    rows = MIX_ROWS
    n_seq = MIX_SEQS
    assert batch % n_seq == 0 and (m // batch) % rows == 0
    tblocks = m // batch // rows
    lcat = _chunk_sum_matrix(rows)
    row = lambda p: p.reshape(1, -1).astype(F32)
    row_spec = lambda w: pl.BlockSpec((n_seq, rows, w), lambda b, i: (0, b * tblocks + i, 0))
    vec = _const_spec((1, width))
    act = lambda: pltpu.VMEM((n_seq * rows, width), F32)
    pair_w = 2 * B_HEAD_DIM
    n_pairs = width // pair_w
    n_keys = n_seq * (rows // CHUNK) * n_pairs
    out = pl.pallas_call(
        _rwkv_kernel,
        grid=(batch // n_seq, tblocks),
        in_specs=[row_spec(in_w), _const_spec((1, in_w)), vec, _const_spec(w2.shape), vec,
                  _const_spec(a2.shape), _const_spec(g2.shape), vec, vec, vec, vec, vec,
                  _const_spec(lcat.shape)],
        out_specs=row_spec(width),
        out_shape=jax.ShapeDtypeStruct((n_seq, m // n_seq, width), BF16),
        scratch_shapes=[pltpu.VMEM((n_seq, rows + SUBLANES, in_w), F32),
                        pltpu.VMEM((n_seq * n_pairs, pair_w, pair_w), F32)]
                       + [act() for _ in range(11)]
                       + [pltpu.VMEM((n_keys, 2 * CHUNK, pair_w), BF16), pltpu.VMEM((n_keys, CHUNK, pair_w), F32),
                          pltpu.VMEM((n_keys, CHUNK, 4 * CHUNK), BF16), pltpu.VMEM((n_keys, pair_w, 2 * CHUNK), BF16),
                          pltpu.VMEM((n_keys, pair_w, pair_w), F32)],
        compiler_params=_params(("arbitrary", "arbitrary")),
        name="rwkv_mixer",
    )(_seq_groups(pb, n_seq), row(mu), row(w0), w2, row(a0), a2, g2, row(k_k), row(k_a), row(r_k), row(ln_w),
      row(ln_b), lcat)
    return out.reshape(m, width)


def _rglru_body(gx_ref, cw_ref, cb_ref, wa_ref, ba_ref, wx_ref, bx_ref, lam_ref, o_ref, xp_ref, h_ref):
    rows = gx_ref.shape[0]
    width = o_ref.shape[1]
    xp_ref[SUBLANES:SUBLANES + rows, :] = gx_ref[:, width:2 * width]
    xc = cb_ref[...] + cw_ref[C_CONV - 1:C_CONV, :] * _shifted_rows(xp_ref, 0, rows)
    for back in range(1, C_CONV):
        xc = xc + cw_ref[C_CONV - 1 - back:C_CONV - back, :] * _shifted_rows(xp_ref, back, rows)
    xp_ref[0:SUBLANES, :] = xp_ref[rows:rows + SUBLANES, :]

    gate_r = _sigmoid(_dot(xc, wa_ref[...]) + ba_ref[...])
    gate_i = _sigmoid(_dot(xc, wx_ref[...]) + bx_ref[...])
    log_a = -C_GATE_SCALE * gate_r * _softplus(-lam_ref[...])
    a = jnp.exp(log_a)
    d = jnp.sqrt(jnp.tanh(-log_a) * (a * a + 1.0)) * (gate_i * xc)

    t_idx = _iota((rows, width), 0)
    shift = 1
    while shift < SUBLANES:
        keep = t_idx >= shift
        d = jnp.where(keep, a * pltpu.roll(d, shift, 0) + d, d)
        a = jnp.where(keep, a * pltpu.roll(a, shift, 0), a)
        shift *= 2
    while shift < rows:
        d = jnp.concatenate([d[:shift], a[shift:] * d[:rows - shift] + d[shift:]], axis=0)
        a = jnp.concatenate([a[:shift], a[shift:] * a[:rows - shift]], axis=0)
        shift *= 2
    h = a * h_ref[0:1, :] + d
    h_ref[...] = jnp.broadcast_to(h[rows - 1:rows, :], h_ref.shape)
    o_ref[...] = (h * _gelu_tanh(gx_ref[:, 0:width])).astype(o_ref.dtype)


_LOG_GAMMA = tuple(math.log(1.0 - 2.0 ** (-5.0 - h)) for h in range(D_HEADS))


def _per_head(lane_head, values):
    out = jnp.full(lane_head.shape, values[-1], F32)
    for h in range(len(values) - 2, -1, -1):
        out = jnp.where(lane_head == h, values[h], out)
    return out


def _retention_body(qk_ref, vg_ref, cos_ref, sin_ref, gw_ref, gb_ref, o_ref, s_ref, dm_ref):
    rows = qk_ref.shape[0]
    kw = qk_ref.shape[1] // 2
    vw = o_ref.shape[1]

    def rotary(x):
        even = _mod(_iota((rows, LANES), 1), 2) == 0
        parts = []
        for c in range(kw // LANES):
            xc = x[:, c * LANES:(c + 1) * LANES]
            parts.append(jnp.where(even, pltpu.roll(xc, LANES - 1, 1), pltpu.roll(xc, 1, 1)))
        return x * cos_ref[...] + jnp.concatenate(parts, axis=1) * sin_ref[...]

    q = rotary(qk_ref[:, 0:kw])
    k = rotary(qk_ref[:, kw:2 * kw]) * (D_KEY_DIM ** -0.5)
    v = vg_ref[:, 0:vw]
    k_head = _div(_iota((1, kw), 1), D_KEY_DIM)
    lg_k = _per_head(k_head, _LOG_GAMMA)
    lg_v = _per_head(_div(_iota((1, vw), 1), D_VAL_DIM), _LOG_GAMMA)
    t_idx = _iota((rows, 1), 0).astype(F32)

    o_cross = _dot(q * jnp.exp((t_idx + 1.0) * lg_k), s_ref[...])
    inner = []
    for h in range(D_HEADS):
        scores = _dot(jnp.where(k_head == h, q, 0.0), k, NT) * dm_ref[h]
        inner.append(_dot(scores, v[:, h * D_VAL_DIM:(h + 1) * D_VAL_DIM]))
    o = o_cross + jnp.concatenate(inner, axis=1)

    k_dec = k * jnp.exp((rows - 1.0 - t_idx) * lg_k)
    same_head = _div(_iota((kw, vw), 0), D_KEY_DIM) == _div(_iota((kw, vw), 1), D_VAL_DIM)
    s_ref[...] = s_ref[...] * jnp.exp(float(rows) * lg_v) + jnp.where(same_head, _dot(k_dec, v, TN), 0.0)

    inv_n = 1.0 / D_VAL_DIM
    mu = _head_sums(o, D_VAL_DIM) * inv_n
    oc = o - mu
    var = _head_sums(oc * oc, D_VAL_DIM) * inv_n
    on = oc * lax.rsqrt(var + EPS) * gw_ref[...] + gb_ref[...]
    o_ref[...] = (_silu(vg_ref[:, vw:2 * vw]) * on).astype(o_ref.dtype)


def _mixer_cd_kernel(gx_ref, cw_ref, cb_ref, wa_ref, ba_ref, wx_ref, bx_ref, lam_ref, qk_ref, vg_ref, cos_ref,
                     sin_ref, gw_ref, gb_ref, oc_ref, od_ref, xp_ref, h_ref, s_ref, dm_ref):
    rows = qk_ref.shape[0]

    @pl.when((pl.program_id(0) == 0) & (pl.program_id(1) == 0))
    def _():
        rel = (_iota((rows, rows), 0) - _iota((rows, rows), 1)).astype(F32)
        for h in range(D_HEADS):
            dm_ref[h] = jnp.where(rel >= 0, jnp.exp(jnp.maximum(rel, 0.0) * _LOG_GAMMA[h]), 0.0)

    @pl.when(pl.program_id(1) == 0)
    def _():
        xp_ref[0:SUBLANES, :] = jnp.zeros((SUBLANES, xp_ref.shape[1]), F32)
        h_ref[...] = jnp.zeros(h_ref.shape, F32)
        s_ref[...] = jnp.zeros(s_ref.shape, F32)

    _retention_body(qk_ref, vg_ref, cos_ref, sin_ref, gw_ref, gb_ref, od_ref, s_ref, dm_ref)
    _rglru_body(gx_ref, cw_ref, cb_ref, wa_ref, ba_ref, wx_ref, bx_ref, lam_ref, oc_ref, xp_ref, h_ref)


def _mixer_cd(gx, qk, vg, conv_w, conv_b, wa, ba, wx, bx, lam, gn_w, gn_b, batch):
    m = qk.shape[0]
    cw = gx.shape[1] // 2
    kw = qk.shape[1] // 2
    vw = vg.shape[1] // 2
    rows = MIX_ROWS
    seq = m // batch
    tblocks = seq // rows
    inv = 1.0 / (ROPE_BASE ** jnp.linspace(0.0, 1.0, D_KEY_DIM // 2, dtype=F32))
    ang = jnp.arange(seq).astype(F32)[:, None] * inv[None, :]
    cos = jnp.tile(jnp.repeat(jnp.cos(ang), 2, axis=1), (1, kw // D_KEY_DIM))
    sin = jnp.tile(jnp.stack([-jnp.sin(ang), jnp.sin(ang)], axis=-1).reshape(seq, D_KEY_DIM),
                   (1, kw // D_KEY_DIM))
    row = lambda p: p.reshape(1, -1).astype(F32)
    blockdiag = lambda w: jax.scipy.linalg.block_diag(*[w[i] for i in range(w.shape[0])]).astype(BF16)
    row_spec = lambda w: pl.BlockSpec((rows, w), lambda b, i: (b * tblocks + i, 0))
    tab_spec = pl.BlockSpec((rows, kw), lambda b, i: (i, 0))
    cvec, cmat, vvec = _const_spec((1, cw)), _const_spec((cw, cw)), _const_spec((1, vw))
    return pl.pallas_call(
        _mixer_cd_kernel,
        grid=(batch, tblocks),
        in_specs=[row_spec(2 * cw), _const_spec(conv_w.shape), cvec, cmat, cvec, cmat, cvec, cvec,
                  row_spec(2 * kw), row_spec(2 * vw), tab_spec, tab_spec, vvec, vvec],
        out_specs=[row_spec(cw), row_spec(vw)],
        out_shape=[jax.ShapeDtypeStruct((m, cw), BF16), jax.ShapeDtypeStruct((m, vw), BF16)],
        scratch_shapes=[pltpu.VMEM((rows + SUBLANES, cw), F32), pltpu.VMEM((SUBLANES, cw), F32),
                        pltpu.VMEM((kw, vw), F32), pltpu.VMEM((D_HEADS, rows, rows), F32)],
        compiler_params=_params(("arbitrary", "arbitrary")),
        name="rglru_retention_mixer",
    )(gx, conv_w, row(conv_b), blockdiag(wa), row(ba), blockdiag(wx), row(bx), row(lam),
      qk, vg, cos, sin, row(gn_w), row(gn_b))


def _ffn_kernel(x_ref, xh_ref, oa_ref, oah_ref, ob_ref, obh_ref, wo_ref, g_ref, wup_ref, wc_ref, wd_ref,
                *rest, seq_blocks, final):
    if final:
        gf_ref, y_ref, hc_ref, u_ref = rest
    else:
        y_ref, hc_ref, u_ref = rest
    rows = x_ref.shape[0]
    half = oa_ref.shape[1]
    ffn = wd_ref.shape[0]

    def mixed(x, oa, ob):
        return (x + jnp.dot(oa, wo_ref[0:half, :], preferred_element_type=F32)
                + jnp.dot(ob, wo_ref[half:2 * half, :], preferred_element_type=F32))

    x1 = mixed(x_ref[...], oa_ref[...], ob_ref[...])
    x1h = mixed(xh_ref[...], oah_ref[...], obh_ref[...])
    starts_seq = (pl.program_id(0) % seq_blocks) == 0
    hc_ref[0:BF16_ROWS, :] = (_rms(x1h, g_ref[...]) * jnp.where(starts_seq, 0.0, 1.0)).astype(BF16)
    hc_ref[BF16_ROWS:BF16_ROWS + rows, :] = _rms(x1, g_ref[...]).astype(BF16)

    u_ref[...] = jnp.dot(hc_ref[...], wup_ref[:, 0:ffn], preferred_element_type=F32)
    v = jnp.dot(hc_ref[BF16_ROWS:BF16_ROWS + rows, :], wup_ref[:, ffn:2 * ffn], preferred_element_type=F32)
    uc = wc_ref[FFN_CONV - 1:FFN_CONV, :] * u_ref[pl.ds(BF16_ROWS, rows), :]
    for back in range(1, FFN_CONV):
        uc = uc + wc_ref[FFN_CONV - 1 - back:FFN_CONV - back, :] * u_ref[pl.ds(BF16_ROWS - back, rows), :]
    gl = (_silu(uc) * v).astype(BF16)
    y = x1 + jnp.dot(gl, wd_ref[...], preferred_element_type=F32)
    if final:
        y = _rms(y, gf_ref[...])
    y_ref[...] = y


def _ffn(x, oa, ob, w_out, norm_w, w_up, w_conv, w_down, seq, final_norm=None):
    m, d = x.shape
    half = oa.shape[1]
    ffn = w_down.shape[0]
    tm = FFN_ROWS
    halo = BF16_ROWS
    final = final_norm is not None
    row_spec = lambda w: pl.BlockSpec((tm, w), lambda i: (i, 0))
    halo_spec = lambda w: pl.BlockSpec((halo, w), lambda i: (jnp.maximum(i * (tm // halo) - 1, 0), 0))
    resident = lambda w: pl.BlockSpec(w.shape, lambda i: (0, 0), pipeline_mode=pl.Buffered(1))
    row = lambda p: p.reshape(1, -1).astype(F32)
    in_specs = [row_spec(d), halo_spec(d), row_spec(half), halo_spec(half), row_spec(half), halo_spec(half),
                resident(w_out), _const_spec((1, d)), resident(w_up), _const_spec(w_conv.shape), resident(w_down)]
    args = [x, x, oa, oa, ob, ob, w_out, row(norm_w), w_up, w_conv, w_down]
    if final:
        in_specs.append(_const_spec((1, d)))
        args.append(row(final_norm))
    return pl.pallas_call(
        functools.partial(_ffn_kernel, seq_blocks=seq // tm, final=final),
        grid=(m // tm,),
        in_specs=in_specs,
        out_specs=row_spec(d),
        out_shape=jax.ShapeDtypeStruct((m, d), F32),
        scratch_shapes=[pltpu.VMEM((tm + halo, d), BF16), pltpu.VMEM((tm + halo, ffn), F32)],
        compiler_params=_params(("parallel",), [a is w_out or a is w_up or a is w_down for a in args]),
        name="outproj_ffn_final" if final else "outproj_ffn",
    )(*args)


def kernel(x, l0_norm1, l0_w_in, l0_a_conv, l0_a_A_log, l0_a_dt_bias, l0_a_norm, l0_b_mu, l0_b_w0, l0_b_w2, l0_b_a0, l0_b_a2, l0_b_g2, l0_b_k_k, l0_b_k_a, l0_b_r_k, l0_b_ln_w, l0_b_ln_b, l0_w_out, l0_norm2, l0_ffn_up, l0_ffn_conv, l0_ffn_down, l1_norm1, l1_w_in, l1_c_conv_w, l1_c_conv_b, l1_c_wa, l1_c_ba, l1_c_wx, l1_c_bx, l1_c_lambda, l1_d_gn_w, l1_d_gn_b, l1_w_out, l1_norm2, l1_ffn_up, l1_ffn_conv, l1_ffn_down, final_norm):
    batch, seq, d = x.shape
    xf = x.reshape(batch * seq, d)
    bf = lambda w: w.astype(BF16)

    a_w = l0_a_conv.shape[1] // 3
    a_heads = l0_a_A_log.shape[0]
    a_in = 4 * a_w + 2 * a_heads
    w_gates = jnp.concatenate([bf(l0_w_in[:, 4 * a_w:a_in]), jnp.zeros((d, LANES - 2 * a_heads), BF16)], axis=1)
    b_in = l0_w_in.shape[1] - a_in
    qkvz, ba, pb = _norm_proj(xf, l0_norm1, [bf(l0_w_in[:, 0:4 * a_w]), w_gates, bf(l0_w_in[:, a_in:])],
                              ((4 * a_w,), (LANES,), (b_in,)), "norm_proj0")
    o_a = _gdn(qkvz, ba, l0_a_conv, l0_a_A_log, l0_a_dt_bias, l0_a_norm, batch)
    o_bb = _rwkv(pb, l0_b_mu, l0_b_w0, bf(l0_b_w2), l0_b_a0, bf(l0_b_a2), bf(l0_b_g2),
                 l0_b_k_k, l0_b_k_a, l0_b_r_k, l0_b_ln_w, l0_b_ln_b, batch)
    x1 = _ffn(xf, o_a, o_bb, bf(l0_w_out), l0_norm2, bf(l0_ffn_up), l0_ffn_conv, bf(l0_ffn_down), seq)

    c_w = l1_c_lambda.shape[0]
    d_w = l1_d_gn_w.shape[0]
    qk_w = l1_w_in.shape[1] - 2 * c_w - 2 * d_w
    gx, qk, vg = _norm_proj(x1, l1_norm1, [bf(l1_w_in)], ((2 * c_w, qk_w, 2 * d_w),), "norm_proj1")
    o_c, o_d = _mixer_cd(gx, qk, vg, l1_c_conv_w, l1_c_conv_b, l1_c_wa, l1_c_ba, l1_c_wx, l1_c_bx, l1_c_lambda,
                         l1_d_gn_w, l1_d_gn_b, batch)
    y = _ffn(x1, o_c, o_d, bf(l1_w_out), l1_norm2, bf(l1_ffn_up), l1_ffn_conv, bf(l1_ffn_down), seq,
             final_norm=final_norm)
    return y.reshape(batch, seq, d)
```
